```python
import math
import jax
import jax.numpy as jnp
from jax import lax
import numpy as np

D_MODEL = 1024
BATCH = 16
SEQ = 2048
DEPTH = 1
DEC_BATCH = 128
DEC_SEQ = 1
PAST_LEN = 8192
PAGE_SIZE = 128

ATT_WIDTH = D_MODEL // 2
N_HEADS = 8
HEAD_DIM = ATT_WIDTH // N_HEADS
SSM_WIDTH = D_MODEL - ATT_WIDTH
SSM_GROUP = 16
N_GROUPS = SSM_WIDTH // SSM_GROUP
STATE_DIM = 64
N_META = 16
BLOCK = 128
N_EXPERTS = 32
TOP_K = 4
D_FF = D_MODEL
SWIGLU_LIMIT = 7.0
SWIGLU_ALPHA = 1.702
EPS = 1e-5
DT_MIN = 1e-3
DT_MAX = 1e-1
ATT_SCALE = 1.0 / math.sqrt(HEAD_DIM)
SB_BIAS_INIT = -7.0

kernel_name = 'hymba_stickbreak_s5_moe_step'


def _rms_norm(x, g):
    xf = x.astype(jnp.float32)
    y = xf * lax.rsqrt(jnp.mean(xf * xf, axis=-1, keepdims=True) + EPS)
    return (y * g.astype(jnp.float32)).astype(x.dtype)


def _project(h, w_in):
    b, t = h.shape[0], h.shape[1]
    p = h @ w_in
    q = p[..., :ATT_WIDTH].reshape(b, t, N_HEADS, HEAD_DIM)
    k = p[..., ATT_WIDTH:2 * ATT_WIDTH].reshape(b, t, N_HEADS, HEAD_DIM)
    v = p[..., 2 * ATT_WIDTH:3 * ATT_WIDTH].reshape(b, t, N_HEADS, HEAD_DIM)
    u = p[..., 3 * ATT_WIDTH:].reshape(b, t, N_GROUPS, SSM_GROUP)
    return q, k, v, u


def _sb_weights(z, valid):
    log_keep = jnp.where(valid, jax.nn.log_sigmoid(-z), 0.0)
    suffix = lax.cumsum(log_keep, axis=z.ndim - 1, reverse=True) - log_keep
    return jnp.where(valid, jnp.exp(jax.nn.log_sigmoid(z) + suffix), 0.0)


def _sb_attention_prompt(q, k, v, bias):
    t = q.shape[1]
    pad = (-t) % BLOCK
    cfg = ((0, 0), (pad, 0), (0, 0), (0, 0))
    qp, kp, vp = jnp.pad(q, cfg), jnp.pad(k, cfg), jnp.pad(v, cfg)
    t_pad = t + pad
    bh = bias.astype(jnp.float32)[None, :, None, None]
    outs = []
    for blk in range(t_pad // BLOCK):
        s0, s1 = blk * BLOCK, (blk + 1) * BLOCK
        z = jnp.einsum('bqhd,bkhd->bhqk', qp[:, s0:s1], kp[:, :s1]).astype(jnp.float32) * ATT_SCALE + bh
        qpos = jnp.arange(s0, s1)[:, None]
        kpos = jnp.arange(s1)[None, :]
        valid = (kpos < qpos) & (kpos >= pad)
        a = _sb_weights(z, valid)
        outs.append(jnp.einsum('bhqk,bkhd->bqhd', a.astype(vp.dtype), vp[:, :s1]))
    return jnp.concatenate(outs, axis=1)[:, pad:]


def _sb_attention_sample(q, k_new, v_new, k_past, v_past, bias):
    p, ds = k_past.shape[1], q.shape[1]
    z = jnp.concatenate([
        jnp.einsum('bqhd,bkhd->bhqk', q, k_past),
        jnp.einsum('bqhd,bkhd->bhqk', q, k_new)], axis=-1).astype(jnp.float32) * ATT_SCALE
    z = z + bias.astype(jnp.float32)[None, :, None, None]
    qpos = p + jnp.arange(ds)[:, None]
    kpos = jnp.arange(p + ds)[None, :]
    a = _sb_weights(z, kpos < qpos).astype(v_past.dtype)
    return (jnp.einsum('bhqk,bkhd->bqhd', a[..., :p], v_past)
            + jnp.einsum('bhqk,bkhd->bqhd', a[..., p:], v_new))


def _s5_discretise(lam_re, lam_im, log_dt, b_re, b_im):
    f = jnp.float32
    lr, li = lam_re.astype(f), lam_im.astype(f)
    dt = jnp.exp(log_dt.astype(f))[:, None]
    mag = jnp.exp(lr * dt)
    ang = li * dt
    ab_re, ab_im = mag * jnp.cos(ang), mag * jnp.sin(ang)
    den = lr * lr + li * li
    num_re = ab_re - 1.0
    coef_re = (num_re * lr + ab_im * li) / den
    coef_im = (ab_im * lr - num_re * li) / den
    br, bi = b_re.astype(f), b_im.astype(f)
    bb_re = coef_re[..., None] * br - coef_im[..., None] * bi
    bb_im = coef_re[..., None] * bi + coef_im[..., None] * br
    return ab_re, ab_im, bb_re, bb_im


def _complex_scan_combine(e1, e2):
    a1r, a1i, b1r, b1i = e1
    a2r, a2i, b2r, b2i = e2
    return (a1r * a2r - a1i * a2i,
            a1r * a2i + a1i * a2r,
            a2r * b1r - a2i * b1i + b2r,
            a2r * b1i + a2i * b1r + b2i)


def _s5_branch(u, h0_re, h0_im, disc, c_re, c_im, d_skip, w_glu, b_glu):
    ab_re, ab_im, bb_re, bb_im = disc
    f = jnp.float32
    uf = u.astype(f)
    b, t = u.shape[0], u.shape[1]
    bu_re = jnp.einsum('gnc,btgc->btgn', bb_re, uf)
    bu_im = jnp.einsum('gnc,btgc->btgn', bb_im, uf)
    h0r, h0i = h0_re.astype(f), h0_im.astype(f)
    bu_re = bu_re.at[:, 0].add(ab_re * h0r - ab_im * h0i)
    bu_im = bu_im.at[:, 0].add(ab_re * h0i + ab_im * h0r)
    a_re = jnp.broadcast_to(ab_re, bu_re.shape)
    a_im = jnp.broadcast_to(ab_im, bu_im.shape)
    _, _, xr, xi = lax.associative_scan(_complex_scan_combine, (a_re, a_im, bu_re, bu_im), axis=1)
    y = (jnp.einsum('gcn,btgn->btgc', c_re.astype(f), xr)
         - jnp.einsum('gcn,btgn->btgc', c_im.astype(f), xi)
         + d_skip.astype(f) * uf).reshape(b, t, SSM_WIDTH)
    y = jax.nn.gelu(y)
    y = y * jax.nn.sigmoid(y @ w_glu.astype(f) + b_glu.astype(f))
    return y, xr[:, -1], xi[:, -1]


def _merge(att, ssm, g_att, g_ssm, w_out, dtype):
    b, t = att.shape[0], att.shape[1]
    a = _rms_norm(att.reshape(b, t, ATT_WIDTH).astype(jnp.float32), g_att)
    s = _rms_norm(ssm, g_ssm)
    return (jnp.concatenate([a, s], axis=-1) @ w_out).astype(dtype)


def _moe(x, w_router, b_router, w_gate, b_gate, w_up, b_up, w_down, b_down):
    logits = (x @ w_router + b_router).astype(jnp.float32)
    top_val, top_idx = lax.top_k(logits, TOP_K)
    top_w = jax.nn.softmax(top_val, axis=-1)
    gates = jnp.sum(jax.nn.one_hot(top_idx, N_EXPERTS, dtype=jnp.float32) * top_w[..., None], axis=1)
    out = jnp.zeros(x.shape, jnp.float32)
    for e in range(N_EXPERTS):
        g = jnp.minimum(x @ w_gate[e] + b_gate[e], SWIGLU_LIMIT)
        up = jnp.clip(x @ w_up[e] + b_up[e], -SWIGLU_LIMIT, SWIGLU_LIMIT)
        act = g * jax.nn.sigmoid(SWIGLU_ALPHA * g) * (up + 1.0)
        out = out + gates[:, e:e + 1] * (act @ w_down[e] + b_down[e])
    return out.astype(x.dtype)


def setup_inputs(seed: int = 0) -> dict:
    key = jax.random.key(seed)
    ks = jax.random.split(key, 35)
    f = jnp.float32
    n_pages = PAST_LEN // PAGE_SIZE
    n_phys = (DEC_BATCH * n_pages * 5) // 4

    def nrm(k, shape, s):
        return jax.random.normal(k, shape, f) * s

    kv_shape = (DEPTH, n_phys, PAGE_SIZE, N_HEADS, HEAD_DIM)
    st_shape = (DEPTH, DEC_BATCH, N_GROUPS, STATE_DIM)
    lam_shape = (DEPTH, N_GROUPS, STATE_DIM)
    page_table = jax.random.permutation(ks[4], n_phys)[:DEC_BATCH * n_pages]
    page_table = page_table.reshape(DEC_BATCH, n_pages).astype(jnp.int32)
    return {
        'x_prompt': nrm(ks[0], (BATCH, SEQ, D_MODEL), 1.0),
        'x_sample': nrm(ks[1], (DEC_BATCH, DEC_SEQ, D_MODEL), 1.0),
        'cache_k': nrm(ks[2], kv_shape, 1.0),
        'cache_v': nrm(ks[3], kv_shape, 1.0),
        'page_table': page_table,
        'state_ssm_re': nrm(ks[5], st_shape, 0.5),
        'state_ssm_im': nrm(ks[6], st_shape, 0.5),
        'meta_tokens': nrm(ks[7], (N_META, D_MODEL), 1.0),
        'g_mix': 1.0 + nrm(ks[8], (DEPTH, D_MODEL), 0.01),
        'w_in': nrm(ks[9], (DEPTH, D_MODEL, 3 * ATT_WIDTH + SSM_WIDTH), D_MODEL ** -0.5),
        'sb_bias': SB_BIAS_INIT + nrm(ks[33], (DEPTH, N_HEADS), 0.1),
        'lambda_re': -0.5 + nrm(ks[10], lam_shape, 0.01),
        'lambda_im': math.pi * jnp.arange(STATE_DIM, dtype=f) + nrm(ks[11], lam_shape, 0.01),
        'log_dt': jax.random.uniform(ks[12], (DEPTH, N_GROUPS), f, math.log(DT_MIN), math.log(DT_MAX)),
        'b_re': nrm(ks[13], (DEPTH, N_GROUPS, STATE_DIM, SSM_GROUP), (2 * SSM_GROUP) ** -0.5),
        'b_im': nrm(ks[14], (DEPTH, N_GROUPS, STATE_DIM, SSM_GROUP), (2 * SSM_GROUP) ** -0.5),
        'c_re': nrm(ks[15], (DEPTH, N_GROUPS, SSM_GROUP, STATE_DIM), (2 * STATE_DIM) ** -0.5),
        'c_im': nrm(ks[16], (DEPTH, N_GROUPS, SSM_GROUP, STATE_DIM), (2 * STATE_DIM) ** -0.5),
        'd_skip': nrm(ks[17], (DEPTH, N_GROUPS, SSM_GROUP), 1.0),
        'w_glu': nrm(ks[18], (DEPTH, SSM_WIDTH, SSM_WIDTH), SSM_WIDTH ** -0.5),
        'b_glu': nrm(ks[19], (DEPTH, SSM_WIDTH), 0.01),
        'g_att_out': 1.0 + nrm(ks[20], (DEPTH, ATT_WIDTH), 0.01),
        'g_ssm_out': 1.0 + nrm(ks[21], (DEPTH, SSM_WIDTH), 0.01),
        'w_out': nrm(ks[22], (DEPTH, D_MODEL, D_MODEL), D_MODEL ** -0.5),
        'g_ffn': 1.0 + nrm(ks[23], (DEPTH, D_MODEL), 0.01),
        'w_router': nrm(ks[24], (DEPTH, D_MODEL, N_EXPERTS), D_MODEL ** -0.5),
        'b_router': nrm(ks[25], (DEPTH, N_EXPERTS), 0.01),
        'w_gate': nrm(ks[26], (DEPTH, N_EXPERTS, D_MODEL, D_FF), D_MODEL ** -0.5),
        'b_gate': nrm(ks[27], (DEPTH, N_EXPERTS, D_FF), 0.01),
        'w_up': nrm(ks[28], (DEPTH, N_EXPERTS, D_MODEL, D_FF), D_MODEL ** -0.5),
        'b_up': nrm(ks[29], (DEPTH, N_EXPERTS, D_FF), 0.01),
        'w_down': nrm(ks[30], (DEPTH, N_EXPERTS, D_FF, D_MODEL), D_FF ** -0.5),
        'b_down': nrm(ks[31], (DEPTH, N_EXPERTS, D_MODEL), 0.01),
        'g_final': 1.0 + nrm(ks[32], (D_MODEL,), 0.01),
    }


def reference(x_prompt, x_sample, cache_k, cache_v, page_table, state_ssm_re, state_ssm_im,
              meta_tokens, g_mix, w_in, sb_bias, lambda_re, lambda_im, log_dt, b_re, b_im, c_re, c_im,
              d_skip, w_glu, b_glu, g_att_out, g_ssm_out, w_out, g_ffn, w_router, b_router,
              w_gate, b_gate, w_up, b_up, w_down, b_down, g_final):
    b = x_prompt.shape[0]
    db, ds = x_sample.shape[0], x_sample.shape[1]
    n_pages = page_table.shape[1]
    meta = jnp.broadcast_to(meta_tokens[None].astype(x_prompt.dtype), (b, N_META, D_MODEL))
    h_p = jnp.concatenate([meta, x_prompt], axis=1)
    h_s = x_sample
    t = h_p.shape[1]
    nk_p, nv_p, nk_s, nv_s = [], [], [], []
    sr_p, si_p, sr_s, si_s = [], [], [], []
    for l in range(DEPTH):
        q_p, k_p, v_p, u_p = _project(_rms_norm(h_p, g_mix[l]), w_in[l])
        q_s, k_s, v_s, u_s = _project(_rms_norm(h_s, g_mix[l]), w_in[l])
        att_p = _sb_attention_prompt(q_p, k_p, v_p, sb_bias[l])
        k_past = cache_k[l][page_table].reshape(db, n_pages * PAGE_SIZE, N_HEADS, HEAD_DIM)
        v_past = cache_v[l][page_table].reshape(db, n_pages * PAGE_SIZE, N_HEADS, HEAD_DIM)
        att_s = _sb_attention_sample(q_s, k_s, v_s, k_past, v_past, sb_bias[l])
        disc = _s5_discretise(lambda_re[l], lambda_im[l], log_dt[l], b_re[l], b_im[l])
        zeros = jnp.zeros((b, N_GROUPS, STATE_DIM), jnp.float32)
        ssm_p, hr_p, hi_p = _s5_branch(u_p, zeros, zeros, disc, c_re[l], c_im[l], d_skip[l], w_glu[l], b_glu[l])
        ssm_s, hr_s, hi_s = _s5_branch(u_s, state_ssm_re[l], state_ssm_im[l], disc, c_re[l], c_im[l],
                                       d_skip[l], w_glu[l], b_glu[l])
        h_p = h_p + _merge(att_p, ssm_p, g_att_out[l], g_ssm_out[l], w_out[l], h_p.dtype)
        h_s = h_s + _merge(att_s, ssm_s, g_att_out[l], g_ssm_out[l], w_out[l], h_s.dtype)
        f_in = jnp.concatenate([_rms_norm(h_p, g_ffn[l]).reshape(-1, D_MODEL),
                                _rms_norm(h_s, g_ffn[l]).reshape(-1, D_MODEL)], axis=0)
        f_out = _moe(f_in, w_router[l], b_router[l], w_gate[l], b_gate[l], w_up[l], b_up[l],
                     w_down[l], b_down[l])
        h_p = h_p + f_out[:b * t].reshape(b, t, D_MODEL)
        h_s = h_s + f_out[b * t:].reshape(db, ds, D_MODEL)
        nk_p.append(k_p)
        nv_p.append(v_p)
        nk_s.append(k_s)
        nv_s.append(v_s)
        sr_p.append(hr_p.astype(state_ssm_re.dtype))
        si_p.append(hi_p.astype(state_ssm_im.dtype))
        sr_s.append(hr_s.astype(state_ssm_re.dtype))
        si_s.append(hi_s.astype(state_ssm_im.dtype))
    y_prompt = _rms_norm(h_p, g_final)[:, N_META:]
    y_sample = _rms_norm(h_s, g_final)
    return (y_prompt, y_sample, jnp.stack(nk_p), jnp.stack(nv_p), jnp.stack(nk_s), jnp.stack(nv_s),
            jnp.stack(sr_p), jnp.stack(si_p), jnp.stack(sr_s), jnp.stack(si_s))
```

```python
import functools
import math

import jax
import jax.numpy as jnp
from jax import lax
from jax.experimental import pallas as pl
from jax.experimental.pallas import tpu as pltpu

F32 = jnp.float32
BF16 = jnp.bfloat16
I32 = jnp.int32

D_MODEL = 1024
ATT_WIDTH = 512
N_HEADS = 8
HEAD_DIM = 64
SSM_WIDTH = 512
SSM_GROUP = 16
N_GROUPS = 32
STATE_DIM = 64
N_EXPERTS = 32
TOP_K = 4
SWIGLU_LIMIT = 7.0
SWIGLU_ALPHA = 1.702
EPS = 1e-5
ATT_SCALE = 1.0 / math.sqrt(HEAD_DIM)
BLOCK = 128

LANES = 128
SUBLANES = 8
ROW_TILES = D_MODEL // LANES
HALF_GROUPS = N_GROUPS // 2
HALF_IN = HALF_GROUPS * SSM_GROUP
HALF_STATE = HALF_GROUPS * STATE_DIM
VMEM_LIMIT = 56 * 1024 * 1024

PROJ_T = 32
S5_T = 32
MERGE_T = SUBLANES
PAGES_PER_STEP = 8
EXPERT_TILE = 256


def _rms(x, g):
    return x * lax.rsqrt(jnp.mean(x * x, axis=-1, keepdims=True) + EPS) * g


def _dot(a, b, **kw):
    return jnp.dot(a, b, preferred_element_type=F32, **kw)


def _dot_nt(a, b):
    return lax.dot_general(a, b, (((1,), (1,)), ((), ())), preferred_element_type=F32)


def _params(sem=None):
    return pltpu.CompilerParams(dimension_semantics=sem, vmem_limit_bytes=VMEM_LIMIT)


def _proj_body(x_ref, g_ref, w_ref, q_ref, k_ref, v_ref, kb_ref, vb_ref, u_ref, *, nb, tt):
    x = x_ref[...].reshape(nb * tt, D_MODEL)
    h = _rms(x, g_ref[...]).astype(BF16)
    p = _dot(h, w_ref[...])
    q = p[:, :ATT_WIDTH] * ATT_SCALE
    k = p[:, ATT_WIDTH:2 * ATT_WIDTH]
    v = p[:, 2 * ATT_WIDTH:3 * ATT_WIDTH]
    u = p[:, 3 * ATT_WIDTH:]
    q_ref[...] = q.astype(BF16).reshape(nb, tt, ATT_WIDTH)
    k_ref[...] = k.reshape(nb, tt, ATT_WIDTH)
    v_ref[...] = v.reshape(nb, tt, ATT_WIDTH)
    kb_ref[...] = k.astype(BF16).reshape(nb, tt, ATT_WIDTH)
    vb_ref[...] = v.astype(BF16).reshape(nb, tt, ATT_WIDTH)
    for b in range(nb):
        u_ref[:, b, :] = u[b * tt:(b + 1) * tt, :]


def _project(x, g, w, tt):
    nb, t, _ = x.shape
    spec_x = pl.BlockSpec((nb, tt, D_MODEL), lambda i: (0, i, 0))
    spec_a = pl.BlockSpec((nb, tt, ATT_WIDTH), lambda i: (0, i, 0))
    shp = lambda dt: jax.ShapeDtypeStruct((nb, t, ATT_WIDTH), dt)
    return pl.pallas_call(
        functools.partial(_proj_body, nb=nb, tt=tt),
        grid=(t // tt,),
        in_specs=[spec_x,
                  pl.BlockSpec((1, D_MODEL), lambda i: (0, 0)),
                  pl.BlockSpec(w.shape, lambda i: (0, 0))],
        out_specs=[spec_a, spec_a, spec_a, spec_a, spec_a,
                   pl.BlockSpec((tt, nb, SSM_WIDTH), lambda i: (i, 0, 0))],
        out_shape=[shp(BF16), shp(F32), shp(F32), shp(BF16), shp(BF16),
                   jax.ShapeDtypeStruct((t, nb, SSM_WIDTH), F32)],
        compiler_params=_params(("arbitrary",)),
        name="proj",
    )(x, g, w)


def _softplus(z):
    return jnp.maximum(z, 0.0) + jnp.log1p(jnp.exp(-jnp.abs(z)))


def _sb_block(z, valid, carry, tri):
    sp = _softplus(z)
    if valid is not None:
        sp = jnp.where(valid, sp, 0.0)
    hi = sp.astype(BF16)
    lo = (sp - hi.astype(F32)).astype(BF16)
    suffix = _dot(hi, tri) + _dot(lo, tri)
    a = jnp.exp(z - suffix - carry)
    if valid is not None:
        a = jnp.where(valid, a, 0.0)
    return a.astype(BF16), carry + suffix[:, 0:1]


def _suffix_matrix():
    j = lax.broadcasted_iota(I32, (BLOCK, BLOCK), 0)
    s = lax.broadcasted_iota(I32, (BLOCK, BLOCK), 1)
    return (j >= s).astype(BF16)


def _attn_prompt_body(bias_ref, q_ref, k_ref, v_ref, km_ref, vm_ref, o_ref, *, n_meta):
    jq = pl.program_id(1)
    tri = _suffix_matrix()
    row = lax.broadcasted_iota(I32, (BLOCK, BLOCK), 0)
    col = lax.broadcasted_iota(I32, (BLOCK, LANES), 1)
    causal = col < row
    meta_valid = col >= BLOCK - n_meta
    low_half = col < HEAD_DIM
    outs = []
    for p in range(N_HEADS // 2):
        ls = slice(p * LANES, (p + 1) * LANES)
        q_pair = q_ref[0, :, ls]
        qm = [jnp.where(low_half, q_pair, jnp.zeros_like(q_pair)),
              jnp.where(low_half, jnp.zeros_like(q_pair), q_pair)]
        bias = [bias_ref[2 * p], bias_ref[2 * p + 1]]

        def block(kblk, vblk, valid, state):
            new = []
            for hh in range(2):
                carry, acc = state[hh]
                z = _dot_nt(qm[hh], kblk) + bias[hh]
                a, carry = _sb_block(z, valid, carry, tri)
                new.append((carry, acc + _dot(a, vblk)))
            return tuple(new)

        zero = (jnp.zeros((BLOCK, 1), F32), jnp.zeros((BLOCK, LANES), F32))
        start = pl.multiple_of(jq * BLOCK, BLOCK)
        state = block(k_ref[0, pl.ds(start, BLOCK), ls], v_ref[0, pl.ds(start, BLOCK), ls],
                      causal, (zero, zero))

        def older(i, state):
            s0 = pl.multiple_of((jq - 1 - i) * BLOCK, BLOCK)
            return block(k_ref[0, pl.ds(s0, BLOCK), ls], v_ref[0, pl.ds(s0, BLOCK), ls], None, state)

        state = lax.fori_loop(0, jq, older, state)
        state = block(km_ref[:, ls], vm_ref[:, ls], meta_valid, state)
        outs.append(jnp.where(low_half, state[0][1], state[1][1]))
    o_ref[0] = jnp.concatenate(outs, axis=1)


def _attn_prompt(bias, q, k, v, km, vm, n_meta):
    b, t, _ = q.shape
    return pl.pallas_call(
        functools.partial(_attn_prompt_body, n_meta=n_meta),
        grid=(b, t // BLOCK),
        in_specs=[pl.BlockSpec(memory_space=pltpu.SMEM),
                  pl.BlockSpec((1, BLOCK, ATT_WIDTH), lambda i, j: (i, j, 0)),
                  pl.BlockSpec((1, t, ATT_WIDTH), lambda i, j: (i, 0, 0)),
                  pl.BlockSpec((1, t, ATT_WIDTH), lambda i, j: (i, 0, 0)),
                  pl.BlockSpec((BLOCK, ATT_WIDTH), lambda i, j: (0, 0)),
                  pl.BlockSpec((BLOCK, ATT_WIDTH), lambda i, j: (0, 0))],
        out_specs=pl.BlockSpec((1, BLOCK, ATT_WIDTH), lambda i, j: (i, j, 0)),
        out_shape=jax.ShapeDtypeStruct((b, t, ATT_WIDTH), F32),
        compiler_params=_params(("arbitrary", "arbitrary")),
        name="attn_prompt",
    )(bias, q, k, v, km, vm)


def _attn_sample_body(pt_ref, q_ref, bias_ref, *refs, pps):
    k_refs, v_refs = refs[:pps], refs[pps:2 * pps]
    o_ref, carry_ref, acc_ref = refs[2 * pps:]
    c = pl.program_id(1)

    @pl.when(c == 0)
    def _():
        carry_ref[...] = jnp.zeros_like(carry_ref)
        acc_ref[...] = jnp.zeros_like(acc_ref)

    tri = _suffix_matrix()
    q = q_ref[0]
    carry = carry_ref[...][:, 0:1]
    acc = acc_ref[...]
    for i in reversed(range(pps)):
        z = _dot_nt(q, k_refs[i][0].astype(BF16)) + bias_ref[...]
        a, carry = _sb_block(z, None, carry, tri)
        acc = acc + _dot(a, v_refs[i][0].astype(BF16))
    carry_ref[...] = jnp.broadcast_to(carry, carry_ref.shape)
    acc_ref[...] = acc

    @pl.when(c == pl.num_programs(1) - 1)
    def _():
        head = lax.broadcasted_iota(I32, acc.shape, 0)
        lane = lax.broadcasted_iota(I32, acc.shape, 1)
        own = lane // HEAD_DIM == head
        o_ref[0] = jnp.sum(jnp.where(own, acc, 0.0), axis=0, keepdims=True)


def _attn_sample(page_table, q_bd, bias, cache_k, cache_v):
    db, n_pages = page_table.shape
    page = cache_k.shape[1]
    pps = min(PAGES_PER_STEP, n_pages)
    n_chunks = n_pages // pps

    def page_spec(i):
        def index(b, c, pt):
            return (pt[b * n_pages + (n_chunks - 1 - c) * pps + i], 0, 0)
        return pl.BlockSpec((1, page, ATT_WIDTH), index)

    grid_spec = pltpu.PrefetchScalarGridSpec(
        num_scalar_prefetch=1,
        grid=(db, n_chunks),
        in_specs=[pl.BlockSpec((1, N_HEADS, ATT_WIDTH), lambda b, c, pt: (b, 0, 0)),
                  pl.BlockSpec((N_HEADS, 1), lambda b, c, pt: (0, 0))]
                 + [page_spec(i) for i in range(pps)] * 2,
        out_specs=pl.BlockSpec((1, 1, ATT_WIDTH), lambda b, c, pt: (b, 0, 0)),
        scratch_shapes=[pltpu.VMEM((N_HEADS, LANES), F32), pltpu.VMEM((N_HEADS, ATT_WIDTH), F32)],
    )
    return pl.pallas_call(
        functools.partial(_attn_sample_body, pps=pps),
        grid_spec=grid_spec,
        out_shape=jax.ShapeDtypeStruct((db, 1, ATT_WIDTH), F32),
        compiler_params=_params(("arbitrary", "arbitrary")),
        name="attn_sample",
    )(page_table.reshape(-1), q_bd, bias, *([cache_k] * pps), *([cache_v] * pps))


def _s5_prep_body(lr_ref, li_ref, ldt_ref, br_ref, bi_ref, ar_ref, ai_ref, bbr_ref, bbi_ref):
    lr, li = lr_ref[...], li_ref[...]
    dt = jnp.exp(ldt_ref[...])
    mag = jnp.exp(lr * dt)
    ang = li * dt
    ab_re, ab_im = mag * jnp.cos(ang), mag * jnp.sin(ang)
    den = lr * lr + li * li
    num_re = ab_re - 1.0
    coef_re = (num_re * lr + ab_im * li) / den
    coef_im = (ab_im * lr - num_re * li) / den
    br, bi = br_ref[...], bi_ref[...]
    ar_ref[...] = ab_re
    ai_ref[...] = ab_im
    bbr_ref[...] = coef_re * br - coef_im * bi
    bbi_ref[...] = coef_re * bi + coef_im * br


def _s5_prep(lam_re, lam_im, log_dt, b_re, b_im):
    g, n = lam_re.shape
    v3 = lambda a: a.reshape(g, 1, n)
    bt = lambda a: jnp.swapaxes(a, 1, 2)
    small = jax.ShapeDtypeStruct((g, 1, n), F32)
    big = jax.ShapeDtypeStruct((g, SSM_GROUP, n), F32)
    return pl.pallas_call(
        _s5_prep_body, out_shape=[small, small, big, big], name="s5_prep",
    )(v3(lam_re), v3(lam_im), jnp.broadcast_to(log_dt[:, None, None], (g, 1, n)), bt(b_re), bt(b_im))


def _s5_body(u_ref, um_ref, h0_ref, bbd_ref, cre_ref, cim_ref, are_ref, aim_ref, d_ref,
             wglu_ref, bglu_ref, g_ref, s_ref, hf_ref, st_ref, bu_ref, *, tc, nb, n_meta, lane_w):
    i = pl.program_id(0)
    rows = tc * nb
    precision = lax.Precision.HIGHEST if bbd_ref.dtype == F32 else None

    def drive(uh, h):
        return _dot(uh.astype(bbd_ref.dtype), bbd_ref[h], precision=precision)

    @pl.when(i == 0)
    def _():
        st_ref[...] = h0_ref[...]
        for h in range(2 if n_meta else 0):
            bum = drive(um_ref[:, h * HALF_IN:(h + 1) * HALF_IN], h)
            ar, ai = are_ref[h], aim_ref[h]
            xr, xi = st_ref[h, 0], st_ref[h, 1]
            for t in range(n_meta):
                br, bi = bum[t:t + 1, :HALF_STATE], bum[t:t + 1, HALF_STATE:]
                xr, xi = ar * xr - ai * xi + br, ar * xi + ai * xr + bi
            st_ref[h, 0] = xr
            st_ref[h, 1] = xi

    u = u_ref[...].reshape(rows, SSM_WIDTH)
    ys = []
    for h in range(2):
        bu = drive(u[:, h * HALF_IN:(h + 1) * HALF_IN], h)
        bu_ref[0] = bu[:, :HALF_STATE]
        bu_ref[1] = bu[:, HALF_STATE:]
        for lc in range(HALF_STATE // lane_w):
            sl = slice(lc * lane_w, (lc + 1) * lane_w)
            ar = jnp.broadcast_to(are_ref[h, :, sl], (nb, lane_w))
            ai = jnp.broadcast_to(aim_ref[h, :, sl], (nb, lane_w))

            def step(t, x, sl=sl, ar=ar, ai=ai):
                xr, xi = x
                r0 = pl.multiple_of(t * nb, nb)
                nxr = ar * xr - ai * xi + bu_ref[0, pl.ds(r0, nb), sl]
                nxi = ar * xi + ai * xr + bu_ref[1, pl.ds(r0, nb), sl]
                bu_ref[0, pl.ds(r0, nb), sl] = nxr
                bu_ref[1, pl.ds(r0, nb), sl] = nxi
                return nxr, nxi

            xr, xi = lax.fori_loop(0, tc, step, (st_ref[h, 0, :, sl], st_ref[h, 1, :, sl]))
            st_ref[h, 0, :, sl] = xr
            st_ref[h, 1, :, sl] = xi
        ys.append(_dot(bu_ref[0].astype(BF16), cre_ref[h]) - _dot(bu_ref[1].astype(BF16), cim_ref[h]))
    y = jnp.concatenate(ys, axis=1) + d_ref[...] * u
    y = jax.nn.gelu(y)
    y = y * jax.nn.sigmoid(_dot(y.astype(BF16), wglu_ref[...]) + bglu_ref[...])
    s_ref[...] = _rms(y, g_ref[...]).reshape(tc, nb, SSM_WIDTH)

    @pl.when(i == pl.num_programs(0) - 1)
    def _():
        hf_ref[...] = st_ref[...]


def _s5(u, u_meta, h0, bbd, cre, cim, a_re, a_im, d, w_glu, b_glu, g):
    t, nb, _ = u.shape
    tc = min(S5_T, t)
    n_meta = 0 if u_meta is None else u_meta.shape[0]
    if u_meta is None:
        u_meta = jnp.zeros((SUBLANES, SSM_WIDTH), F32)
    lane_w = max(LANES, min(4 * LANES, (64 * LANES) // nb))
    full = lambda a: pl.BlockSpec(a.shape, lambda i: (0,) * a.ndim)
    blk = pl.BlockSpec((tc, nb, SSM_WIDTH), lambda i: (i, 0, 0))
    args = (u, u_meta, h0, bbd, cre, cim, a_re, a_im, d, w_glu, b_glu, g)
    return pl.pallas_call(
        functools.partial(_s5_body, tc=tc, nb=nb, n_meta=n_meta, lane_w=lane_w),
        grid=(t // tc,),
        in_specs=[blk] + [full(a) for a in args[1:]],
        out_specs=[blk, full(h0)],
        out_shape=[jax.ShapeDtypeStruct(u.shape, F32), jax.ShapeDtypeStruct(h0.shape, F32)],
        scratch_shapes=[pltpu.VMEM(h0.shape, F32), pltpu.VMEM((2, tc * nb, HALF_STATE), F32)],
        compiler_params=_params(("arbitrary",)),
        name="s5",
    )(*args)


def _block_diag_in(bb):
    eye = jnp.eye(HALF_GROUPS, dtype=F32)
    bb = bb.reshape(2, HALF_GROUPS, SSM_GROUP, STATE_DIM)
    return jnp.einsum('hgcn,gk->hgckn', bb, eye).reshape(2, HALF_IN, HALF_STATE)


def _block_diag_out(c):
    eye = jnp.eye(HALF_GROUPS, dtype=F32)
    c = c.reshape(2, HALF_GROUPS, SSM_GROUP, STATE_DIM)
    return jnp.einsum('hgcn,kg->hkngc', c, eye).reshape(2, HALF_STATE, HALF_IN)


def _state_in(s):
    nb = s.shape[0]
    return jnp.swapaxes(s.reshape(nb, 2, HALF_STATE), 0, 1)


def _state_out(s):
    nb = s.shape[1]
    return jnp.swapaxes(s, 0, 1).reshape(nb, N_GROUPS, STATE_DIM)


def _merge_body(xp_ref, ap_ref, sp_ref, xs_ref, as_ref, ss_ref, gatt_ref, wout_ref, gffn_ref,
                wr_ref, br_ref, h1_ref, f3_ref, idx_ref, wts_ref, rank_ref, cnt_ref, run_ref,
                *, nb, n_blk):
    i = pl.program_id(0)
    rows = nb * MERGE_T
    is_sample = i == n_blk

    @pl.when(i == 0)
    def _():
        run_ref[...] = jnp.zeros_like(run_ref)

    s_bt = jnp.concatenate([sp_ref[:, b, :] for b in range(nb)], axis=0)
    x = jnp.where(is_sample, xs_ref[...], xp_ref[...].reshape(rows, D_MODEL))
    att = jnp.where(is_sample, as_ref[...], ap_ref[...].reshape(rows, ATT_WIDTH))
    ssm = jnp.where(is_sample, ss_ref[...], s_bt)
    cat = jnp.concatenate([_rms(att, gatt_ref[...]), ssm], axis=1).astype(BF16)
    h1 = x + _dot(cat, wout_ref[...])
    f = _rms(h1, gffn_ref[...])
    h1_ref[...] = h1
    for c in range(ROW_TILES):
        f3_ref[:, c, :] = f[:, c * LANES:(c + 1) * LANES]

    logits = _dot(f, wr_ref[...], precision=lax.Precision.HIGHEST) + br_ref[...]
    lane = lax.broadcasted_iota(I32, logits.shape, 1)
    vals, ids, hots = [], [], []
    for _ in range(TOP_K):
        m = jnp.max(logits, axis=1, keepdims=True)
        sel = jnp.min(jnp.where(logits == m, lane, N_EXPERTS), axis=1, keepdims=True)
        hot = lane == sel
        vals.append(m)
        ids.append(sel)
        hots.append(hot)
        logits = jnp.where(hot, -jnp.inf, logits)
    es = [jnp.exp(v - vals[0]) for v in vals]
    den = es[0] + es[1] + es[2] + es[3]

    chosen = (hots[0] | hots[1] | hots[2] | hots[3])
    tr = lax.broadcasted_iota(I32, (rows, rows), 0)
    tc = lax.broadcasted_iota(I32, (rows, rows), 1)
    before = _dot((tc < tr).astype(BF16), chosen.astype(BF16)) + run_ref[...]
    run_ref[...] = run_ref[...] + jnp.sum(chosen.astype(F32), axis=0, keepdims=True)
    cnt_ref[...] = run_ref[...]

    out_lane = lax.broadcasted_iota(I32, (rows, LANES), 1)
    idx_o = jnp.zeros((rows, LANES), I32)
    wts_o = jnp.zeros((rows, LANES), F32)
    rank_o = jnp.zeros((rows, LANES), I32)
    for k in range(TOP_K):
        rank_k = jnp.sum(jnp.where(hots[k], before, 0.0), axis=1, keepdims=True).astype(I32)
        idx_o = jnp.where(out_lane == k, ids[k], idx_o)
        wts_o = jnp.where(out_lane == k, es[k] / den, wts_o)
        rank_o = jnp.where(out_lane == k, rank_k, rank_o)
    idx_ref[...] = idx_o
    wts_ref[...] = wts_o
    rank_ref[...] = rank_o


def _merge(x_p, att_p, s_p, x_s, att_s, s_s, g_att, w_out, g_ffn, w_r, b_r):
    nb, t, _ = x_p.shape
    rows = nb * MERGE_T
    n_blk = t // MERGE_T
    n_tok = (n_blk + 1) * rows
    pj = lambda i: jnp.minimum(i, n_blk - 1)
    full = lambda a: pl.BlockSpec(a.shape, lambda i: (0,) * a.ndim)
    row_spec = lambda w: pl.BlockSpec((rows, w), lambda i: (i, 0))
    return pl.pallas_call(
        functools.partial(_merge_body, nb=nb, n_blk=n_blk),
        grid=(n_blk + 1,),
        in_specs=[pl.BlockSpec((nb, MERGE_T, D_MODEL), lambda i: (0, pj(i), 0)),
                  pl.BlockSpec((nb, MERGE_T, ATT_WIDTH), lambda i: (0, pj(i), 0)),
                  pl.BlockSpec((MERGE_T, nb, SSM_WIDTH), lambda i: (pj(i), 0, 0)),
                  full(x_s), full(att_s), full(s_s), full(g_att), full(w_out), full(g_ffn),
                  full(w_r), full(b_r)],
        out_specs=[row_spec(D_MODEL),
                   pl.BlockSpec((rows, ROW_TILES, LANES), lambda i: (i, 0, 0)),
                   row_spec(LANES), row_spec(LANES), row_spec(LANES),
                   pl.BlockSpec((1, N_EXPERTS), lambda i: (0, 0))],
        out_shape=[jax.ShapeDtypeStruct((n_tok, D_MODEL), F32),
                   jax.ShapeDtypeStruct((n_tok, ROW_TILES, LANES), F32),
                   jax.ShapeDtypeStruct((n_tok, LANES), I32),
                   jax.ShapeDtypeStruct((n_tok, LANES), F32),
                   jax.ShapeDtypeStruct((n_tok, LANES), I32),
                   jax.ShapeDtypeStruct((1, N_EXPERTS), F32)],
        scratch_shapes=[pltpu.VMEM((1, N_EXPERTS), F32)],
        compiler_params=_params(("arbitrary",)),
        name="merge_route",
    )(x_p, att_p, s_p, x_s, att_s, s_s, g_att, w_out, g_ffn, w_r, b_r)


def _dispatch_body(tail_ref, has_ref, nval_ref, pos_ref, f3_ref, o_hbm, zero_ref, sem,
                   *, rows, tile, n_tiles):
    i = pl.program_id(0)

    def tile_copy(start):
        return pltpu.make_async_copy(zero_ref, o_hbm.at[pl.ds(start, tile)], sem)

    @pl.when(i == 0)
    def _():
        zero_ref[...] = jnp.zeros_like(zero_ref)

        def start(e, c):
            @pl.when(has_ref[e] > 0)
            def _():
                tile_copy(tail_ref[e]).start()
            return c

        def wait(e, c):
            @pl.when(has_ref[e] > 0)
            def _():
                tile_copy(tail_ref[e]).wait()
            return c

        def start_unused(j, c):
            tile_copy(pl.multiple_of(j * tile, tile)).start()
            return c

        def wait_unused(j, c):
            tile_copy(pl.multiple_of(j * tile, tile)).wait()
            return c

        lax.fori_loop(0, N_EXPERTS, start, 0)
        lax.fori_loop(nval_ref[0], n_tiles, start_unused, 0)
        lax.fori_loop(0, N_EXPERTS, wait, 0)
        lax.fori_loop(nval_ref[0], n_tiles, wait_unused, 0)

    def issue(r, c):
        for k in range(TOP_K):
            pltpu.make_async_copy(f3_ref.at[r], o_hbm.at[pos_ref[r * TOP_K + k]], sem).start()
        return c

    lax.fori_loop(0, rows, issue, 0)
    for _ in range(TOP_K):
        pltpu.make_async_copy(f3_ref, o_hbm.at[pl.ds(0, rows)], sem).wait()


def _dispatch(tail_start, has_rows, n_valid, pos, f3, rows, m_tot, tile):
    n_tok = f3.shape[0]
    grid_spec = pltpu.PrefetchScalarGridSpec(
        num_scalar_prefetch=3,
        grid=(n_tok // rows,),
        in_specs=[pl.BlockSpec((rows * TOP_K,), lambda i, *_: (i,), memory_space=pltpu.SMEM),
                  pl.BlockSpec((rows, ROW_TILES, LANES), lambda i, *_: (i, 0, 0))],
        out_specs=pl.BlockSpec(memory_space=pl.ANY),
        scratch_shapes=[pltpu.VMEM((tile, ROW_TILES, LANES), F32), pltpu.SemaphoreType.DMA(())],
    )
    return pl.pallas_call(
        functools.partial(_dispatch_body, rows=rows, tile=tile, n_tiles=m_tot // tile),
        grid_spec=grid_spec,
        out_shape=jax.ShapeDtypeStruct((m_tot, ROW_TILES, LANES), F32),
        compiler_params=_params(("arbitrary",)),
        name="dispatch",
    )(tail_start, has_rows, n_valid, pos, f3)


def _experts_body(tidx_ref, te_ref, nval_ref, x_ref, wg_ref, wu_ref, wd_ref, bg_ref, bu_ref, bd_ref,
                  y_ref, wg_s, wu_s, wd_s):
    i = pl.program_id(0)
    e = te_ref[i]
    prev = te_ref[jnp.maximum(i - 1, 0)]

    @pl.when((i == 0) | (e != prev))
    def _():
        wg_s[...] = wg_ref[0].astype(BF16)
        wu_s[...] = wu_ref[0].astype(BF16)
        wd_s[...] = wd_ref[0].astype(BF16)

    @pl.when(i < nval_ref[0])
    def _():
        x = jnp.concatenate([x_ref[:, c, :] for c in range(ROW_TILES)], axis=1).astype(BF16)
        g = jnp.minimum(_dot(x, wg_s[...]) + bg_ref[0], SWIGLU_LIMIT)
        up = jnp.clip(_dot(x, wu_s[...]) + bu_ref[0], -SWIGLU_LIMIT, SWIGLU_LIMIT)
        act = g * jax.nn.sigmoid(SWIGLU_ALPHA * g) * (up + 1.0)
        y = _dot(act.astype(BF16), wd_s[...]) + bd_ref[0]
        for c in range(ROW_TILES):
            y_ref[:, c, :] = y[:, c * LANES:(c + 1) * LANES]

    @pl.when(i >= nval_ref[0])
    def _():
        y_ref[...] = jnp.zeros_like(y_ref)


def _experts(tile_idx, tile_e, n_valid, xs, w_gate, w_up, w_down, b_gate, b_up, b_down, tile):
    n_tiles = tile_idx.shape[0]
    d_ff = w_gate.shape[-1]
    row = pl.BlockSpec((tile, ROW_TILES, LANES), lambda i, ti, te, nv: (ti[i], 0, 0))
    w_spec = lambda a: pl.BlockSpec((1,) + a.shape[1:], lambda i, ti, te, nv: (te[i], 0, 0))
    b3 = lambda b: b.reshape(N_EXPERTS, 1, b.shape[-1])
    grid_spec = pltpu.PrefetchScalarGridSpec(
        num_scalar_prefetch=3,
        grid=(n_tiles,),
        in_specs=[row, w_spec(w_gate), w_spec(w_up), w_spec(w_down),
                  w_spec(b3(b_gate)), w_spec(b3(b_up)), w_spec(b3(b_down))],
        out_specs=pl.BlockSpec((tile, ROW_TILES, LANES), lambda i, ti, te, nv: (i, 0, 0)),
        scratch_shapes=[pltpu.VMEM((D_MODEL, d_ff), BF16), pltpu.VMEM((D_MODEL, d_ff), BF16),
                        pltpu.VMEM((d_ff, D_MODEL), BF16)],
    )
    return pl.pallas_call(
        _experts_body,
        grid_spec=grid_spec,
        out_shape=jax.ShapeDtypeStruct(xs.shape, F32),
        compiler_params=_params(("arbitrary",)),
        name="experts",
    )(tile_idx, tile_e, n_valid, xs, w_gate, w_up, w_down, b3(b_gate), b3(b_up), b3(b_down))


def _combine_body(pos_ref, wts_ref, h1_ref, g_ref, y_hbm, yp_ref, ys_ref, buf_ref, sem,
                  *, nb, n_blk, rows):
    i = pl.program_id(0)

    def issue(r, c):
        for k in range(TOP_K):
            pltpu.make_async_copy(y_hbm.at[pos_ref[r * TOP_K + k]], buf_ref.at[k, r], sem).start()
        return c

    lax.fori_loop(0, rows, issue, 0)
    for k in range(TOP_K):
        pltpu.make_async_copy(y_hbm.at[pl.ds(0, rows)], buf_ref.at[k], sem).wait()

    w = wts_ref[...]
    parts = []
    for c in range(ROW_TILES):
        parts.append(sum(w[:, k:k + 1] * buf_ref[k, :, c, :] for k in range(TOP_K)))
    h = h1_ref[...] + jnp.concatenate(parts, axis=1)
    y = _rms(h, g_ref[...])

    @pl.when(i < n_blk)
    def _():
        yp_ref[...] = y.reshape(nb, MERGE_T, D_MODEL)

    @pl.when(i == n_blk)
    def _():
        ys_ref[...] = y


def _combine(pos, wts, h1, g, ys, nb, t):
    rows = nb * MERGE_T
    n_blk = t // MERGE_T
    return pl.pallas_call(
        functools.partial(_combine_body, nb=nb, n_blk=n_blk, rows=rows),
        grid=(n_blk + 1,),
        in_specs=[pl.BlockSpec((rows * TOP_K,), lambda i: (i,), memory_space=pltpu.SMEM),
                  pl.BlockSpec((rows, LANES), lambda i: (i, 0)),
                  pl.BlockSpec((rows, D_MODEL), lambda i: (i, 0)),
                  pl.BlockSpec((1, D_MODEL), lambda i: (0, 0)),
                  pl.BlockSpec(memory_space=pl.ANY)],
        out_specs=[pl.BlockSpec((nb, MERGE_T, D_MODEL), lambda i: (0, jnp.minimum(i, n_blk - 1), 0)),
                   pl.BlockSpec((rows, D_MODEL), lambda i: (0, 0))],
        out_shape=[jax.ShapeDtypeStruct((nb, t, D_MODEL), F32),
                   jax.ShapeDtypeStruct((rows, D_MODEL), F32)],
        scratch_shapes=[pltpu.VMEM((TOP_K, rows, ROW_TILES, LANES), F32), pltpu.SemaphoreType.DMA(())],
        compiler_params=_params(("arbitrary",)),
        name="combine",
    )(pos, wts, h1, g, ys)


def _moe_plan(counts, idx, rank, tile):
    n_tok = idx.shape[0]
    n_tiles = (n_tok * TOP_K) // tile + N_EXPERTS
    padded = ((counts + tile - 1) // tile) * tile
    ends = jnp.cumsum(padded)
    off = ends - padded
    pos = (off[idx] + rank).reshape(-1)
    tile_ends = ends // tile
    n_valid = tile_ends[-1]
    tile_idx = jnp.minimum(jnp.arange(n_tiles, dtype=I32), n_valid - 1)
    tile_e = jnp.minimum(jnp.searchsorted(tile_ends, tile_idx, side='right'), N_EXPERTS - 1).astype(I32)
    return pos.astype(I32), tile_idx, tile_e, n_valid.reshape(1).astype(I32), (ends - tile).astype(I32), \
        (counts > 0).astype(I32), n_tiles * tile


def kernel(x_prompt, x_sample, cache_k, cache_v, page_table, state_ssm_re, state_ssm_im, meta_tokens, g_mix, w_in, sb_bias, lambda_re, lambda_im, log_dt, b_re, b_im, c_re, c_im, d_skip, w_glu, b_glu, g_att_out, g_ssm_out, w_out, g_ffn, w_router, b_router, w_gate, b_gate, w_up, b_up, w_down, b_down, g_final):
    depth = g_mix.shape[0]
    assert depth == 1, "single-layer step"
    b, t, _ = x_prompt.shape
    db, ds, _ = x_sample.shape
    n_meta = meta_tokens.shape[0]
    assert ds == 1 and db == b * MERGE_T and t % BLOCK == 0 and n_meta <= BLOCK
    l = 0
    row = lambda a: a.reshape(1, -1)

    w_in_b = w_in[l].astype(BF16)
    g_mix_l = row(g_mix[l])
    q_p, k_p, v_p, kb_p, vb_p, u_p = _project(x_prompt, g_mix_l, w_in_b, PROJ_T)
    _, k_m, v_m, kb_m, vb_m, u_m = _project(meta_tokens[None], g_mix_l, w_in_b, n_meta)
    q_s, k_s, v_s, _, _, u_s = _project(x_sample.reshape(1, db, D_MODEL), g_mix_l, w_in_b, db)

    pad_meta = lambda a: jnp.pad(a[0], ((BLOCK - n_meta, 0), (0, 0)))
    att_p = _attn_prompt(sb_bias[l], q_p, kb_p, vb_p, pad_meta(kb_m), pad_meta(vb_m), n_meta)
    n_phys, page = cache_k.shape[1], cache_k.shape[2]
    head_of_lane = jnp.arange(ATT_WIDTH) // HEAD_DIM
    q_bd = jnp.where(head_of_lane[None, None, :] == jnp.arange(N_HEADS)[None, :, None],
                     q_s[0][:, None, :], jnp.zeros((), BF16))
    att_s = _attn_sample(page_table, q_bd, sb_bias[l].reshape(N_HEADS, 1),
                         cache_k[l].reshape(n_phys, page, ATT_WIDTH),
                         cache_v[l].reshape(n_phys, page, ATT_WIDTH))

    ab_re, ab_im, bb_re, bb_im = _s5_prep(lambda_re[l], lambda_im[l], log_dt[l], b_re[l], b_im[l])
    bbd = jnp.concatenate([_block_diag_in(bb_re), _block_diag_in(bb_im)], axis=2)
    cre = _block_diag_out(c_re[l]).astype(BF16)
    cim = _block_diag_out(c_im[l]).astype(BF16)
    a_re = ab_re.reshape(2, 1, HALF_STATE)
    a_im = ab_im.reshape(2, 1, HALF_STATE)
    s5_w = (cre, cim, a_re, a_im, row(d_skip[l]), w_glu[l].astype(BF16), row(b_glu[l]), row(g_ssm_out[l]))
    zero_state = jnp.zeros((2, 2, b, HALF_STATE), F32)
    s_p, hf_p = _s5(u_p, u_m.reshape(n_meta, SSM_WIDTH), zero_state, bbd.astype(BF16), *s5_w)
    h0_s = jnp.stack([_state_in(state_ssm_re[l]), _state_in(state_ssm_im[l])], axis=1)
    s_s, hf_s = _s5(u_s.reshape(1, db, SSM_WIDTH), None, h0_s, bbd, *s5_w)

    h1, f3, idx, wts, rank, cnt = _merge(
        x_prompt, att_p, s_p, x_sample.reshape(db, D_MODEL), att_s.reshape(db, ATT_WIDTH),
        s_s.reshape(db, SSM_WIDTH), row(g_att_out[l]), w_out[l].astype(BF16), row(g_ffn[l]),
        w_router[l], row(b_router[l]))
    counts = cnt[0].astype(I32)
    pos, tile_idx, tile_e, n_valid, tail_start, has_rows, m_tot = _moe_plan(
        counts, idx[:, :TOP_K], rank[:, :TOP_K], EXPERT_TILE)
    rows = b * MERGE_T
    xs = _dispatch(tail_start, has_rows, n_valid, pos, f3, rows, m_tot, EXPERT_TILE)
    ys = _experts(tile_idx, tile_e, n_valid, xs, w_gate[l], w_up[l], w_down[l],
                  b_gate[l], b_up[l], b_down[l], EXPERT_TILE)
    y_prompt, y_sample = _combine(pos, wts, h1, row(g_final), ys, b, t)

    heads = lambda a: a.reshape(a.shape[:-1] + (N_HEADS, HEAD_DIM))
    with_meta = lambda m, p: heads(jnp.concatenate([jnp.broadcast_to(m, (b,) + m.shape[1:]), p], axis=1))[None]
    return (y_prompt, y_sample.reshape(db, ds, D_MODEL),
            with_meta(k_m, k_p), with_meta(v_m, v_p),
            heads(k_s.reshape(db, ds, ATT_WIDTH))[None], heads(v_s.reshape(db, ds, ATT_WIDTH))[None],
            _state_out(hf_p[:, 0])[None], _state_out(hf_p[:, 1])[None],
            _state_out(hf_s[:, 0])[None], _state_out(hf_s[:, 1])[None])
```

```python
import functools
import math

import jax
import jax.numpy as jnp
from jax import lax
from jax.experimental import pallas as pl
from jax.experimental.pallas import tpu as pltpu

F32 = jnp.float32
BF16 = jnp.bfloat16
I32 = jnp.int32

D_MODEL = 1024
ATT_WIDTH = 512
N_HEADS = 8
HEAD_DIM = 64
SSM_WIDTH = 512
SSM_GROUP = 16
N_GROUPS = 32
STATE_DIM = 64
N_EXPERTS = 32
TOP_K = 4
SWIGLU_LIMIT = 7.0
SWIGLU_ALPHA = 1.702
EPS = 1e-5
ATT_SCALE = 1.0 / math.sqrt(HEAD_DIM)
BLOCK = 128

LANES = 128
SUBLANES = 8
ROW_TILES = D_MODEL // LANES
HALF_GROUPS = N_GROUPS // 2
HALF_IN = HALF_GROUPS * SSM_GROUP
HALF_STATE = HALF_GROUPS * STATE_DIM
VMEM_LIMIT = 56 * 1024 * 1024

PROJ_T = 32
S5_T = 32
MERGE_T = SUBLANES
PAGES_PER_STEP = 8
EXPERT_TILE = 256


def _rms(x, g):
    return x * lax.rsqrt(jnp.mean(x * x, axis=-1, keepdims=True) + EPS) * g


def _dot(a, b, **kw):
    return jnp.dot(a, b, preferred_element_type=F32, **kw)


def _dot_nt(a, b):
    return lax.dot_general(a, b, (((1,), (1,)), ((), ())), preferred_element_type=F32)


def _params(sem=None):
    return pltpu.CompilerParams(dimension_semantics=sem, vmem_limit_bytes=VMEM_LIMIT)


def _tile_rows(ref, r, n=1):
    return ref.at[pl.ds(pl.multiple_of(r * ROW_TILES, ROW_TILES), n * ROW_TILES)]


def _load_rows(ref, n):
    return jnp.concatenate([ref[pl.ds(c, n, stride=ROW_TILES), :] for c in range(ROW_TILES)], axis=1)


def _store_rows(ref, val, n):
    for c in range(ROW_TILES):
        ref[pl.ds(c, n, stride=ROW_TILES), :] = val[:, c * LANES:(c + 1) * LANES]


def _proj_body(x_ref, g_ref, w_ref, q_ref, k_ref, v_ref, kb_ref, vb_ref, u_ref, *, nb, tt):
    x = x_ref[...].reshape(nb * tt, D_MODEL)
    h = _rms(x, g_ref[...]).astype(BF16)
    p = _dot(h, w_ref[...])
    q = p[:, :ATT_WIDTH] * ATT_SCALE
    k = p[:, ATT_WIDTH:2 * ATT_WIDTH]
    v = p[:, 2 * ATT_WIDTH:3 * ATT_WIDTH]
    u = p[:, 3 * ATT_WIDTH:]
    q_ref[...] = q.astype(BF16).reshape(nb, tt, ATT_WIDTH)
    k_ref[...] = k.reshape(nb, tt, ATT_WIDTH)
    v_ref[...] = v.reshape(nb, tt, ATT_WIDTH)
    kb_ref[...] = k.astype(BF16).reshape(nb, tt, ATT_WIDTH)
    vb_ref[...] = v.astype(BF16).reshape(nb, tt, ATT_WIDTH)
    for b in range(nb):
        u_ref[:, b, :] = u[b * tt:(b + 1) * tt, :]


def _project(x, g, w, tt):
    nb, t, _ = x.shape
    spec_x = pl.BlockSpec((nb, tt, D_MODEL), lambda i: (0, i, 0))
    spec_a = pl.BlockSpec((nb, tt, ATT_WIDTH), lambda i: (0, i, 0))
    shp = lambda dt: jax.ShapeDtypeStruct((nb, t, ATT_WIDTH), dt)
    return pl.pallas_call(
        functools.partial(_proj_body, nb=nb, tt=tt),
        grid=(t // tt,),
        in_specs=[spec_x,
                  pl.BlockSpec((1, D_MODEL), lambda i: (0, 0)),
                  pl.BlockSpec(w.shape, lambda i: (0, 0))],
        out_specs=[spec_a, spec_a, spec_a, spec_a, spec_a,
                   pl.BlockSpec((tt, nb, SSM_WIDTH), lambda i: (i, 0, 0))],
        out_shape=[shp(BF16), shp(F32), shp(F32), shp(BF16), shp(BF16),
                   jax.ShapeDtypeStruct((t, nb, SSM_WIDTH), F32)],
        compiler_params=_params(("arbitrary",)),
        name="proj",
    )(x, g, w)


def _softplus(z):
    return jnp.maximum(z, 0.0) + jnp.log(1.0 + jnp.exp(-jnp.abs(z)))


def _sb_split(z, valid):
    sp = _softplus(z)
    if valid is not None:
        sp = jnp.where(valid, sp, 0.0)
    hi = sp.astype(BF16)
    lo = (sp - hi.astype(F32)).astype(BF16)
    return jnp.concatenate([hi, lo], axis=1)


def _sb_weights(z, ext, valid, carry):
    a = jnp.exp(z - ext[:, :BLOCK] - carry)
    if valid is not None:
        a = jnp.where(valid, a, 0.0)
    return a.astype(BF16), carry + ext[:, BLOCK:]


def _suffix_matrix():
    j = lax.broadcasted_iota(I32, (2 * BLOCK, 2 * BLOCK), 0)
    s = lax.broadcasted_iota(I32, (2 * BLOCK, 2 * BLOCK), 1)
    j = jnp.where(j >= BLOCK, j - BLOCK, j)
    return ((s >= BLOCK) | (j >= s)).astype(BF16)


def _attn_prompt_body(bias_ref, q_ref, k_ref, v_ref, km_ref, vm_ref, o_ref, qm_ref, carry_ref, acc_ref,
                      *, n_meta):
    jq = pl.program_id(1)
    m2 = _suffix_matrix()
    row = lax.broadcasted_iota(I32, (BLOCK, BLOCK), 0)
    col = lax.broadcasted_iota(I32, (BLOCK, LANES), 1)
    causal = col < row
    meta_valid = col >= BLOCK - n_meta
    low_half = col < HEAD_DIM
    pair_lanes = lambda p: slice(p * LANES, (p + 1) * LANES)

    for h in range(N_HEADS):
        q_pair = q_ref[0, :, pair_lanes(h // 2)]
        qm_ref[h] = jnp.where(low_half == (h % 2 == 0), q_pair, jnp.zeros_like(q_pair))
    carry_ref[...] = jnp.zeros_like(carry_ref)
    acc_ref[...] = jnp.zeros_like(acc_ref)

    def block(k_of, v_of, valid):
        heads = range(N_HEADS)
        zs = [_dot_nt(qm_ref[h], k_of(pair_lanes(h // 2))) + bias_ref[h] for h in heads]
        cats = [_sb_split(z, valid) for z in zs]
        exts = [_dot(cat, m2) for cat in cats]
        pv = []
        for h in heads:
            a, carry = _sb_weights(zs[h], exts[h], valid, carry_ref[h])
            carry_ref[h] = carry
            pv.append(_dot(a, v_of(pair_lanes(h // 2))))
        for p in range(N_HEADS // 2):
            acc_ref[:, pair_lanes(p)] += jnp.where(low_half, pv[2 * p], pv[2 * p + 1])

    def rows_of(ref, start):
        return lambda ls: ref[0, pl.ds(pl.multiple_of(start, BLOCK), BLOCK), ls]

    block(rows_of(k_ref, jq * BLOCK), rows_of(v_ref, jq * BLOCK), causal)

    def older(i, c):
        start = (jq - 1 - i) * BLOCK
        block(rows_of(k_ref, start), rows_of(v_ref, start), None)
        return c

    lax.fori_loop(0, jq, older, 0)
    block(lambda ls: km_ref[:, ls], lambda ls: vm_ref[:, ls], meta_valid)
    o_ref[0] = acc_ref[...]


def _attn_prompt(bias, q, k, v, km, vm, n_meta):
    b, t, _ = q.shape
    return pl.pallas_call(
        functools.partial(_attn_prompt_body, n_meta=n_meta),
        grid=(b, t // BLOCK),
        in_specs=[pl.BlockSpec(memory_space=pltpu.SMEM),
                  pl.BlockSpec((1, BLOCK, ATT_WIDTH), lambda i, j: (i, j, 0)),
                  pl.BlockSpec((1, t, ATT_WIDTH), lambda i, j: (i, 0, 0)),
                  pl.BlockSpec((1, t, ATT_WIDTH), lambda i, j: (i, 0, 0)),
                  pl.BlockSpec((BLOCK, ATT_WIDTH), lambda i, j: (0, 0)),
                  pl.BlockSpec((BLOCK, ATT_WIDTH), lambda i, j: (0, 0))],
        out_specs=pl.BlockSpec((1, BLOCK, ATT_WIDTH), lambda i, j: (i, j, 0)),
        out_shape=jax.ShapeDtypeStruct((b, t, ATT_WIDTH), F32),
        scratch_shapes=[pltpu.VMEM((N_HEADS, BLOCK, LANES), BF16),
                        pltpu.VMEM((N_HEADS, BLOCK, LANES), F32),
                        pltpu.VMEM((BLOCK, ATT_WIDTH), F32)],
        compiler_params=_params(("arbitrary", "arbitrary")),
        name="attn_prompt",
    )(bias, q, k, v, km, vm)


def _attn_sample_body(pt_ref, q_ref, bias_ref, *refs, pps):
    k_refs, v_refs = refs[:pps], refs[pps:2 * pps]
    o_ref, carry_ref, acc_ref = refs[2 * pps:]
    c = pl.program_id(1)

    @pl.when(c == 0)
    def _():
        carry_ref[...] = jnp.zeros_like(carry_ref)
        acc_ref[...] = jnp.zeros_like(acc_ref)

    q = q_ref[0]
    order = list(reversed(range(pps)))
    z = jnp.concatenate([_dot(q, k_refs[i][0].astype(BF16)) + bias_ref[...] for i in order], axis=0)
    ext = _dot(_sb_split(z, None), _suffix_matrix())
    carry = carry_ref[...]
    carries = []
    for j in range(pps):
        carries.append(carry)
        carry = carry + ext[j * N_HEADS:(j + 1) * N_HEADS, BLOCK:]
    a = jnp.exp(z - ext[:, :BLOCK] - jnp.concatenate(carries, axis=0))
    acc = acc_ref[...]
    for j, i in enumerate(order):
        a_j = a[j * N_HEADS:(j + 1) * N_HEADS].astype(BF16)
        acc = acc + _dot_nt(a_j, v_refs[i][0].astype(BF16))
    carry_ref[...] = carry
    acc_ref[...] = acc

    @pl.when(c == pl.num_programs(1) - 1)
    def _():
        head = lax.broadcasted_iota(I32, acc.shape, 0)
        lane = lax.broadcasted_iota(I32, acc.shape, 1)
        own = lane // HEAD_DIM == head
        o_ref[0] = jnp.sum(jnp.where(own, acc, 0.0), axis=0, keepdims=True)


def _attn_sample(page_table, q_bd, bias, cache_k, cache_v):
    db, n_pages = page_table.shape
    page = cache_k.shape[2]
    assert page == BLOCK
    pps = min(PAGES_PER_STEP, n_pages)
    n_chunks = n_pages // pps

    def page_spec(i):
        def index(b, c, pt):
            return (pt[b * n_pages + (n_chunks - 1 - c) * pps + i], 0, 0)
        return pl.BlockSpec((1, ATT_WIDTH, page), index)

    grid_spec = pltpu.PrefetchScalarGridSpec(
        num_scalar_prefetch=1,
        grid=(db, n_chunks),
        in_specs=[pl.BlockSpec((1, N_HEADS, ATT_WIDTH), lambda b, c, pt: (b, 0, 0)),
                  pl.BlockSpec((N_HEADS, 1), lambda b, c, pt: (0, 0))]
                 + [page_spec(i) for i in range(pps)] * 2,
        out_specs=pl.BlockSpec((1, 1, ATT_WIDTH), lambda b, c, pt: (b, 0, 0)),
        scratch_shapes=[pltpu.VMEM((N_HEADS, LANES), F32), pltpu.VMEM((N_HEADS, ATT_WIDTH), F32)],
    )
    return pl.pallas_call(
        functools.partial(_attn_sample_body, pps=pps),
        grid_spec=grid_spec,
        out_shape=jax.ShapeDtypeStruct((db, 1, ATT_WIDTH), F32),
        compiler_params=_params(("arbitrary", "arbitrary")),
        name="attn_sample",
    )(page_table.reshape(-1), q_bd, bias, *([cache_k] * pps), *([cache_v] * pps))


def _s5_prep_body(lr_ref, li_ref, ldt_ref, br_ref, bi_ref, ar_ref, ai_ref, bbr_ref, bbi_ref):
    lr, li = lr_ref[...], li_ref[...]
    dt = jnp.exp(ldt_ref[...])
    mag = jnp.exp(lr * dt)
    ang = li * dt
    ab_re, ab_im = mag * jnp.cos(ang), mag * jnp.sin(ang)
    den = lr * lr + li * li
    num_re = ab_re - 1.0
    coef_re = (num_re * lr + ab_im * li) / den
    coef_im = (ab_im * lr - num_re * li) / den
    br, bi = br_ref[...], bi_ref[...]
    ar_ref[...] = ab_re
    ai_ref[...] = ab_im
    bbr_ref[...] = coef_re * br - coef_im * bi
    bbi_ref[...] = coef_re * bi + coef_im * br


def _s5_prep(lam_re, lam_im, log_dt, b_re, b_im):
    g, n = lam_re.shape
    v3 = lambda a: a.reshape(g, 1, n)
    bt = lambda a: jnp.swapaxes(a, 1, 2)
    small = jax.ShapeDtypeStruct((g, 1, n), F32)
    big = jax.ShapeDtypeStruct((g, SSM_GROUP, n), F32)
    return pl.pallas_call(
        _s5_prep_body, out_shape=[small, small, big, big], name="s5_prep",
    )(v3(lam_re), v3(lam_im), jnp.broadcast_to(log_dt[:, None, None], (g, 1, n)), bt(b_re), bt(b_im))


def _s5_body(u_ref, um_ref, h0_ref, bbd_ref, cre_ref, cim_ref, are_ref, aim_ref, d_ref,
             wglu_ref, bglu_ref, g_ref, s_ref, hf_ref, st_ref, bu_ref, *, tc, nb, n_meta, lane_w):
    i = pl.program_id(0)
    rows = tc * nb
    precision = lax.Precision.HIGHEST if bbd_ref.dtype == F32 else None

    def drive(uh, h):
        return _dot(uh.astype(bbd_ref.dtype), bbd_ref[h], precision=precision)

    @pl.when(i == 0)
    def _():
        st_ref[...] = h0_ref[...]
        for h in range(2 if n_meta else 0):
            bum = drive(um_ref[:, h * HALF_IN:(h + 1) * HALF_IN], h)
            ar, ai = are_ref[h], aim_ref[h]
            xr, xi = st_ref[h, 0], st_ref[h, 1]
            for t in range(n_meta):
                br, bi = bum[t:t + 1, :HALF_STATE], bum[t:t + 1, HALF_STATE:]
                xr, xi = ar * xr - ai * xi + br, ar * xi + ai * xr + bi
            st_ref[h, 0] = xr
            st_ref[h, 1] = xi

    u = u_ref[...].reshape(rows, SSM_WIDTH)
    ys = []
    for h in range(2):
        bu = drive(u[:, h * HALF_IN:(h + 1) * HALF_IN], h)
        bu_ref[0] = bu[:, :HALF_STATE]
        bu_ref[1] = bu[:, HALF_STATE:]
        for lc in range(HALF_STATE // lane_w):
            sl = slice(lc * lane_w, (lc + 1) * lane_w)
            ar = jnp.broadcast_to(are_ref[h, :, sl], (nb, lane_w))
            ai = jnp.broadcast_to(aim_ref[h, :, sl], (nb, lane_w))

            def step(t, x, sl=sl, ar=ar, ai=ai):
                xr, xi = x
                r0 = pl.multiple_of(t * nb, nb)
                nxr = ar * xr - ai * xi + bu_ref[0, pl.ds(r0, nb), sl]
                nxi = ar * xi + ai * xr + bu_ref[1, pl.ds(r0, nb), sl]
                bu_ref[0, pl.ds(r0, nb), sl] = nxr
                bu_ref[1, pl.ds(r0, nb), sl] = nxi
                return nxr, nxi

            xr, xi = lax.fori_loop(0, tc, step, (st_ref[h, 0, :, sl], st_ref[h, 1, :, sl]))
            st_ref[h, 0, :, sl] = xr
            st_ref[h, 1, :, sl] = xi
        ys.append(_dot(bu_ref[0].astype(BF16), cre_ref[h]) - _dot(bu_ref[1].astype(BF16), cim_ref[h]))
    y = jnp.concatenate(ys, axis=1) + d_ref[...] * u
    y = jax.nn.gelu(y)
    y = y * jax.nn.sigmoid(_dot(y.astype(BF16), wglu_ref[...]) + bglu_ref[...])
    s_ref[...] = _rms(y, g_ref[...]).reshape(tc, nb, SSM_WIDTH)

    @pl.when(i == pl.num_programs(0) - 1)
    def _():
        hf_ref[...] = st_ref[...]


def _s5(u, u_meta, h0, bbd, cre, cim, a_re, a_im, d, w_glu, b_glu, g):
    t, nb, _ = u.shape
    tc = min(S5_T, t)
    n_meta = 0 if u_meta is None else u_meta.shape[0]
    if u_meta is None:
        u_meta = jnp.zeros((SUBLANES, SSM_WIDTH), F32)
    lane_w = max(LANES, min(4 * LANES, (64 * LANES) // nb))
    full = lambda a: pl.BlockSpec(a.shape, lambda i: (0,) * a.ndim)
    blk = pl.BlockSpec((tc, nb, SSM_WIDTH), lambda i: (i, 0, 0))
    args = (u, u_meta, h0, bbd, cre, cim, a_re, a_im, d, w_glu, b_glu, g)
    return pl.pallas_call(
        functools.partial(_s5_body, tc=tc, nb=nb, n_meta=n_meta, lane_w=lane_w),
        grid=(t // tc,),
        in_specs=[blk] + [full(a) for a in args[1:]],
        out_specs=[blk, full(h0)],
        out_shape=[jax.ShapeDtypeStruct(u.shape, F32), jax.ShapeDtypeStruct(h0.shape, F32)],
        scratch_shapes=[pltpu.VMEM(h0.shape, F32), pltpu.VMEM((2, tc * nb, HALF_STATE), F32)],
        compiler_params=_params(("arbitrary",)),
        name="s5",
    )(*args)


def _block_diag_in(bb):
    eye = jnp.eye(HALF_GROUPS, dtype=F32)
    bb = bb.reshape(2, HALF_GROUPS, SSM_GROUP, STATE_DIM)
    return jnp.einsum('hgcn,gk->hgckn', bb, eye).reshape(2, HALF_IN, HALF_STATE)


def _block_diag_out(c):
    eye = jnp.eye(HALF_GROUPS, dtype=F32)
    c = c.reshape(2, HALF_GROUPS, SSM_GROUP, STATE_DIM)
    return jnp.einsum('hgcn,kg->hkngc', c, eye).reshape(2, HALF_STATE, HALF_IN)


def _state_in(s):
    nb = s.shape[0]
    return jnp.swapaxes(s.reshape(nb, 2, HALF_STATE), 0, 1)


def _state_out(s):
    nb = s.shape[1]
    return jnp.swapaxes(s, 0, 1).reshape(nb, N_GROUPS, STATE_DIM)


def _merge_body(xp_ref, ap_ref, sp_ref, xs_ref, as_ref, ss_ref, gatt_ref, wout_ref, gffn_ref,
                wr_ref, br_ref, h1_ref, f3_ref, idx_ref, wts_ref, rank_ref, cnt_ref, run_ref,
                *, nb, n_blk):
    i = pl.program_id(0)
    rows = nb * MERGE_T
    is_sample = i == n_blk

    @pl.when(i == 0)
    def _():
        run_ref[...] = jnp.zeros_like(run_ref)

    s_bt = jnp.concatenate([sp_ref[:, b, :] for b in range(nb)], axis=0)
    x = jnp.where(is_sample, xs_ref[...], xp_ref[...].reshape(rows, D_MODEL))
    att = jnp.where(is_sample, as_ref[...], ap_ref[...].reshape(rows, ATT_WIDTH))
    ssm = jnp.where(is_sample, ss_ref[...], s_bt)
    cat = jnp.concatenate([_rms(att, gatt_ref[...]), ssm], axis=1).astype(BF16)
    h1 = x + _dot(cat, wout_ref[...])
    f = _rms(h1, gffn_ref[...])
    h1_ref[...] = h1
    _store_rows(f3_ref, f, rows)

    logits = _dot(f, wr_ref[...], precision=lax.Precision.HIGHEST) + br_ref[...]
    lane = lax.broadcasted_iota(I32, logits.shape, 1)
    vals, ids, hots = [], [], []
    for _ in range(TOP_K):
        m = jnp.max(logits, axis=1, keepdims=True)
        sel = jnp.min(jnp.where(logits == m, lane, N_EXPERTS), axis=1, keepdims=True)
        hot = lane == sel
        vals.append(m)
        ids.append(sel)
        hots.append(hot)
        logits = jnp.where(hot, -jnp.inf, logits)
    es = [jnp.exp(v - vals[0]) for v in vals]
    den = es[0] + es[1] + es[2] + es[3]

    chosen = (hots[0] | hots[1] | hots[2] | hots[3])
    tr = lax.broadcasted_iota(I32, (rows, rows), 0)
    tc = lax.broadcasted_iota(I32, (rows, rows), 1)
    before = _dot((tc < tr).astype(BF16), chosen.astype(BF16)) + run_ref[...]
    run_ref[...] = run_ref[...] + jnp.sum(chosen.astype(F32), axis=0, keepdims=True)
    cnt_ref[...] = run_ref[...]

    out_lane = lax.broadcasted_iota(I32, (rows, LANES), 1)
    idx_o = jnp.zeros((rows, LANES), I32)
    wts_o = jnp.zeros((rows, LANES), F32)
    rank_o = jnp.zeros((rows, LANES), I32)
    for k in range(TOP_K):
        rank_k = jnp.sum(jnp.where(hots[k], before, 0.0), axis=1, keepdims=True).astype(I32)
        idx_o = jnp.where(out_lane == k, ids[k], idx_o)
        wts_o = jnp.where(out_lane == k, es[k] / den, wts_o)
        rank_o = jnp.where(out_lane == k, rank_k, rank_o)
    idx_ref[...] = idx_o
    wts_ref[...] = wts_o
    rank_ref[...] = rank_o


def _merge(x_p, att_p, s_p, x_s, att_s, s_s, g_att, w_out, g_ffn, w_r, b_r):
    nb, t, _ = x_p.shape
    rows = nb * MERGE_T
    n_blk = t // MERGE_T
    n_tok = (n_blk + 1) * rows
    pj = lambda i: jnp.minimum(i, n_blk - 1)
    full = lambda a: pl.BlockSpec(a.shape, lambda i: (0,) * a.ndim)
    row_spec = lambda w: pl.BlockSpec((rows, w), lambda i: (i, 0))
    return pl.pallas_call(
        functools.partial(_merge_body, nb=nb, n_blk=n_blk),
        grid=(n_blk + 1,),
        in_specs=[pl.BlockSpec((nb, MERGE_T, D_MODEL), lambda i: (0, pj(i), 0)),
                  pl.BlockSpec((nb, MERGE_T, ATT_WIDTH), lambda i: (0, pj(i), 0)),
                  pl.BlockSpec((MERGE_T, nb, SSM_WIDTH), lambda i: (pj(i), 0, 0)),
                  full(x_s), full(att_s), full(s_s), full(g_att), full(w_out), full(g_ffn),
                  full(w_r), full(b_r)],
        out_specs=[row_spec(D_MODEL),
                   pl.BlockSpec((rows * ROW_TILES, LANES), lambda i: (i, 0)),
                   row_spec(LANES), row_spec(LANES), row_spec(LANES),
                   pl.BlockSpec((1, N_EXPERTS), lambda i: (0, 0))],
        out_shape=[jax.ShapeDtypeStruct((n_tok, D_MODEL), F32),
                   jax.ShapeDtypeStruct((n_tok * ROW_TILES, LANES), F32),
                   jax.ShapeDtypeStruct((n_tok, LANES), I32),
                   jax.ShapeDtypeStruct((n_tok, LANES), F32),
                   jax.ShapeDtypeStruct((n_tok, LANES), I32),
                   jax.ShapeDtypeStruct((1, N_EXPERTS), F32)],
        scratch_shapes=[pltpu.VMEM((1, N_EXPERTS), F32)],
        compiler_params=_params(("arbitrary",)),
        name="merge_route",
    )(x_p, att_p, s_p, x_s, att_s, s_s, g_att, w_out, g_ffn, w_r, b_r)


def _dispatch_body(tail_ref, has_ref, nval_ref, pos_ref, f3_ref, o_hbm, zero_ref, sem,
                   *, rows, tile, n_tiles):
    i = pl.program_id(0)

    def tile_copy(start):
        return pltpu.make_async_copy(zero_ref, _tile_rows(o_hbm, start, tile), sem)

    @pl.when(i == 0)
    def _():
        zero_ref[...] = jnp.zeros_like(zero_ref)

        def start(e, c):
            @pl.when(has_ref[e] > 0)
            def _():
                tile_copy(tail_ref[e]).start()
            return c

        def wait(e, c):
            @pl.when(has_ref[e] > 0)
            def _():
                tile_copy(tail_ref[e]).wait()
            return c

        def start_unused(j, c):
            tile_copy(pl.multiple_of(j * tile, tile)).start()
            return c

        def wait_unused(j, c):
            tile_copy(pl.multiple_of(j * tile, tile)).wait()
            return c

        lax.fori_loop(0, N_EXPERTS, start, 0)
        lax.fori_loop(nval_ref[0], n_tiles, start_unused, 0)
        lax.fori_loop(0, N_EXPERTS, wait, 0)
        lax.fori_loop(nval_ref[0], n_tiles, wait_unused, 0)

    def issue(r, c):
        for k in range(TOP_K):
            pltpu.make_async_copy(_tile_rows(f3_ref, r), _tile_rows(o_hbm, pos_ref[r * TOP_K + k]), sem).start()
        return c

    lax.fori_loop(0, rows, issue, 0)
    for _ in range(TOP_K):
        pltpu.make_async_copy(f3_ref, _tile_rows(o_hbm, 0, rows), sem).wait()


def _dispatch(tail_start, has_rows, n_valid, pos, f3, rows, m_tot, tile):
    n_tok = f3.shape[0] // ROW_TILES
    grid_spec = pltpu.PrefetchScalarGridSpec(
        num_scalar_prefetch=3,
        grid=(n_tok // rows,),
        in_specs=[pl.BlockSpec((rows * TOP_K,), lambda i, *_: (i,), memory_space=pltpu.SMEM),
                  pl.BlockSpec((rows * ROW_TILES, LANES), lambda i, *_: (i, 0))],
        out_specs=pl.BlockSpec(memory_space=pl.ANY),
        scratch_shapes=[pltpu.VMEM((tile * ROW_TILES, LANES), F32), pltpu.SemaphoreType.DMA(())],
    )
    return pl.pallas_call(
        functools.partial(_dispatch_body, rows=rows, tile=tile, n_tiles=m_tot // tile),
        grid_spec=grid_spec,
        out_shape=jax.ShapeDtypeStruct((m_tot * ROW_TILES, LANES), F32),
        compiler_params=_params(("arbitrary",)),
        name="dispatch",
    )(tail_start, has_rows, n_valid, pos, f3)


def _experts_body(tidx_ref, te_ref, nval_ref, x_ref, wg_ref, wu_ref, wd_ref, bg_ref, bu_ref, bd_ref,
                  y_ref, wg_s, wu_s, wd_s):
    i = pl.program_id(0)
    e = te_ref[i]
    prev = te_ref[jnp.maximum(i - 1, 0)]
    tile = x_ref.shape[0] // ROW_TILES

    @pl.when((i == 0) | (e != prev))
    def _():
        wg_s[...] = wg_ref[0].astype(BF16)
        wu_s[...] = wu_ref[0].astype(BF16)
        wd_s[...] = wd_ref[0].astype(BF16)

    @pl.when(i < nval_ref[0])
    def _():
        x = _load_rows(x_ref, tile).astype(BF16)
        g = jnp.minimum(_dot(x, wg_s[...]) + bg_ref[0], SWIGLU_LIMIT)
        up = jnp.clip(_dot(x, wu_s[...]) + bu_ref[0], -SWIGLU_LIMIT, SWIGLU_LIMIT)
        act = g * jax.nn.sigmoid(SWIGLU_ALPHA * g) * (up + 1.0)
        y = _dot(act.astype(BF16), wd_s[...]) + bd_ref[0]
        _store_rows(y_ref, y, tile)

    @pl.when(i >= nval_ref[0])
    def _():
        y_ref[...] = jnp.zeros_like(y_ref)


def _experts(tile_idx, tile_e, n_valid, xs, w_gate, w_up, w_down, b_gate, b_up, b_down, tile):
    n_tiles = tile_idx.shape[0]
    d_ff = w_gate.shape[-1]
    row = pl.BlockSpec((tile * ROW_TILES, LANES), lambda i, ti, te, nv: (ti[i], 0))
    w_spec = lambda a: pl.BlockSpec((1,) + a.shape[1:], lambda i, ti, te, nv: (te[i], 0, 0))
    b3 = lambda b: b.reshape(N_EXPERTS, 1, b.shape[-1])
    grid_spec = pltpu.PrefetchScalarGridSpec(
        num_scalar_prefetch=3,
        grid=(n_tiles,),
        in_specs=[row, w_spec(w_gate), w_spec(w_up), w_spec(w_down),
                  w_spec(b3(b_gate)), w_spec(b3(b_up)), w_spec(b3(b_down))],
        out_specs=pl.BlockSpec((tile * ROW_TILES, LANES), lambda i, ti, te, nv: (i, 0)),
        scratch_shapes=[pltpu.VMEM((D_MODEL, d_ff), BF16), pltpu.VMEM((D_MODEL, d_ff), BF16),
                        pltpu.VMEM((d_ff, D_MODEL), BF16)],
    )
    return pl.pallas_call(
        _experts_body,
        grid_spec=grid_spec,
        out_shape=jax.ShapeDtypeStruct(xs.shape, F32),
        compiler_params=_params(("arbitrary",)),
        name="experts",
    )(tile_idx, tile_e, n_valid, xs, w_gate, w_up, w_down, b3(b_gate), b3(b_up), b3(b_down))


def _combine_body(pos_ref, next_pos_ref, wts_ref, h1_ref, g_ref, y_hbm, yp_ref, ys_ref, *scratch,
                  nb, n_blk, rows):
    bufs, sem = (scratch[:TOP_K], scratch[TOP_K:2 * TOP_K]), scratch[2 * TOP_K]
    i = pl.program_id(0)

    def gather(p_ref, slot):
        def issue(r, c):
            for k in range(TOP_K):
                pltpu.make_async_copy(_tile_rows(y_hbm, p_ref[r * TOP_K + k]),
                                      _tile_rows(bufs[slot][k], r), sem.at[slot]).start()
            return c
        lax.fori_loop(0, rows, issue, 0)

    def finish(slot):
        for k in range(TOP_K):
            pltpu.make_async_copy(_tile_rows(y_hbm, 0, rows), bufs[slot][k], sem.at[slot]).wait()
        w = wts_ref[...]
        moe = sum(w[:, k:k + 1] * _load_rows(bufs[slot][k], rows) for k in range(TOP_K))
        y = _rms(h1_ref[...] + moe, g_ref[...])

        @pl.when(i < n_blk)
        def _():
            yp_ref[...] = y.reshape(nb, MERGE_T, D_MODEL)

        @pl.when(i == n_blk)
        def _():
            ys_ref[...] = y

    @pl.when(i == 0)
    def _():
        gather(pos_ref, 0)

    for slot in range(2):
        @pl.when(i % 2 == slot)
        def _(slot=slot):
            @pl.when(i < n_blk)
            def _():
                gather(next_pos_ref, 1 - slot)
            finish(slot)


def _combine(pos, wts, h1, g, ys, nb, t):
    rows = nb * MERGE_T
    n_blk = t // MERGE_T
    return pl.pallas_call(
        functools.partial(_combine_body, nb=nb, n_blk=n_blk, rows=rows),
        grid=(n_blk + 1,),
        in_specs=[pl.BlockSpec((rows * TOP_K,), lambda i: (i,), memory_space=pltpu.SMEM),
                  pl.BlockSpec((rows * TOP_K,), lambda i: (jnp.minimum(i + 1, n_blk),),
                               memory_space=pltpu.SMEM),
                  pl.BlockSpec((rows, LANES), lambda i: (i, 0)),
                  pl.BlockSpec((rows, D_MODEL), lambda i: (i, 0)),
                  pl.BlockSpec((1, D_MODEL), lambda i: (0, 0)),
                  pl.BlockSpec(memory_space=pl.ANY)],
        out_specs=[pl.BlockSpec((nb, MERGE_T, D_MODEL), lambda i: (0, jnp.minimum(i, n_blk - 1), 0)),
                   pl.BlockSpec((rows, D_MODEL), lambda i: (0, 0))],
        out_shape=[jax.ShapeDtypeStruct((nb, t, D_MODEL), F32),
                   jax.ShapeDtypeStruct((rows, D_MODEL), F32)],
        scratch_shapes=[pltpu.VMEM((rows * ROW_TILES, LANES), F32)] * (2 * TOP_K)
                       + [pltpu.SemaphoreType.DMA((2,))],
        compiler_params=_params(("arbitrary",)),
        name="combine",
    )(pos, pos, wts, h1, g, ys)


def _moe_plan(counts, idx, rank, tile):
    n_tok = idx.shape[0]
    n_tiles = (n_tok * TOP_K) // tile + N_EXPERTS
    padded = ((counts + tile - 1) // tile) * tile
    ends = jnp.cumsum(padded)
    off = ends - padded
    pos = (off[idx] + rank).reshape(-1)
    tile_ends = ends // tile
    n_valid = tile_ends[-1]
    tile_idx = jnp.minimum(jnp.arange(n_tiles, dtype=I32), n_valid - 1)
    tile_e = jnp.minimum(jnp.sum(tile_idx[:, None] >= tile_ends[None, :], axis=1), N_EXPERTS - 1).astype(I32)
    return pos.astype(I32), tile_idx, tile_e, n_valid.reshape(1).astype(I32), (ends - tile).astype(I32), \
        (counts > 0).astype(I32), n_tiles * tile


def kernel(x_prompt, x_sample, cache_k, cache_v, page_table, state_ssm_re, state_ssm_im, meta_tokens, g_mix, w_in, sb_bias, lambda_re, lambda_im, log_dt, b_re, b_im, c_re, c_im, d_skip, w_glu, b_glu, g_att_out, g_ssm_out, w_out, g_ffn, w_router, b_router, w_gate, b_gate, w_up, b_up, w_down, b_down, g_final):
    depth = g_mix.shape[0]
    assert depth == 1, "single-layer step"
    b, t, _ = x_prompt.shape
    db, ds, _ = x_sample.shape
    n_meta = meta_tokens.shape[0]
    assert ds == 1 and db == b * MERGE_T and t % BLOCK == 0 and n_meta <= BLOCK
    l = 0
    row = lambda a: a.reshape(1, -1)

    w_in_b = w_in[l].astype(BF16)
    g_mix_l = row(g_mix[l])
    q_p, k_p, v_p, kb_p, vb_p, u_p = _project(x_prompt, g_mix_l, w_in_b, PROJ_T)
    _, k_m, v_m, kb_m, vb_m, u_m = _project(meta_tokens[None], g_mix_l, w_in_b, n_meta)
    q_s, k_s, v_s, _, _, u_s = _project(x_sample.reshape(1, db, D_MODEL), g_mix_l, w_in_b, db)

    pad_meta = lambda a: jnp.pad(a[0], ((BLOCK - n_meta, 0), (0, 0)))
    att_p = _attn_prompt(sb_bias[l], q_p, kb_p, vb_p, pad_meta(kb_m), pad_meta(vb_m), n_meta)
    n_phys, page = cache_k.shape[1], cache_k.shape[2]
    head_of_lane = jnp.arange(ATT_WIDTH) // HEAD_DIM
    q_bd = jnp.where(head_of_lane[None, None, :] == jnp.arange(N_HEADS)[None, :, None],
                     q_s[0][:, None, :], jnp.zeros((), BF16))
    keys_minor = lambda c: jnp.transpose(c, (0, 2, 3, 1)).reshape(n_phys, ATT_WIDTH, page)
    att_s = _attn_sample(page_table, q_bd, sb_bias[l].reshape(N_HEADS, 1),
                         keys_minor(cache_k[l]), keys_minor(cache_v[l]))

    ab_re, ab_im, bb_re, bb_im = _s5_prep(lambda_re[l], lambda_im[l], log_dt[l], b_re[l], b_im[l])
    bbd = jnp.concatenate([_block_diag_in(bb_re), _block_diag_in(bb_im)], axis=2)
    cre = _block_diag_out(c_re[l]).astype(BF16)
    cim = _block_diag_out(c_im[l]).astype(BF16)
    a_re = ab_re.reshape(2, 1, HALF_STATE)
    a_im = ab_im.reshape(2, 1, HALF_STATE)
    s5_w = (cre, cim, a_re, a_im, row(d_skip[l]), w_glu[l].astype(BF16), row(b_glu[l]), row(g_ssm_out[l]))
    zero_state = jnp.zeros((2, 2, b, HALF_STATE), F32)
    s_p, hf_p = _s5(u_p, u_m.reshape(n_meta, SSM_WIDTH), zero_state, bbd.astype(BF16), *s5_w)
    h0_s = jnp.stack([_state_in(state_ssm_re[l]), _state_in(state_ssm_im[l])], axis=1)
    s_s, hf_s = _s5(u_s.reshape(1, db, SSM_WIDTH), None, h0_s, bbd, *s5_w)

    h1, f3, idx, wts, rank, cnt = _merge(
        x_prompt, att_p, s_p, x_sample.reshape(db, D_MODEL), att_s.reshape(db, ATT_WIDTH),
        s_s.reshape(db, SSM_WIDTH), row(g_att_out[l]), w_out[l].astype(BF16), row(g_ffn[l]),
        w_router[l], row(b_router[l]))
    counts = cnt[0].astype(I32)
    pos, tile_idx, tile_e, n_valid, tail_start, has_rows, m_tot = _moe_plan(
        counts, idx[:, :TOP_K], rank[:, :TOP_K], EXPERT_TILE)
    rows = b * MERGE_T
    xs = _dispatch(tail_start, has_rows, n_valid, pos, f3, rows, m_tot, EXPERT_TILE)
    ys = _experts(tile_idx, tile_e, n_valid, xs, w_gate[l], w_up[l], w_down[l],
                  b_gate[l], b_up[l], b_down[l], EXPERT_TILE)
    y_prompt, y_sample = _combine(pos, wts, h1, row(g_final), ys, b, t)

    heads = lambda a: a.reshape(a.shape[:-1] + (N_HEADS, HEAD_DIM))
    with_meta = lambda m, p: heads(jnp.concatenate([jnp.broadcast_to(m, (b,) + m.shape[1:]), p], axis=1))[None]
    return (y_prompt, y_sample.reshape(db, ds, D_MODEL),
            with_meta(k_m, k_p), with_meta(v_m, v_p),
            heads(k_s.reshape(db, ds, ATT_WIDTH))[None], heads(v_s.reshape(db, ds, ATT_WIDTH))[None],
            _state_out(hf_p[:, 0])[None], _state_out(hf_p[:, 1])[None],
            _state_out(hf_s[:, 0])[None], _state_out(hf_s[:, 1])[None])
```

```python
import functools
import math

import jax
import jax.numpy as jnp
from jax import lax
from jax.experimental import pallas as pl
from jax.experimental.pallas import tpu as pltpu

F32 = jnp.float32
BF16 = jnp.bfloat16
I32 = jnp.int32

D_MODEL = 1024
ATT_WIDTH = 512
N_HEADS = 8
HEAD_DIM = 64
SSM_WIDTH = 512
SSM_GROUP = 16
N_GROUPS = 32
STATE_DIM = 64
N_EXPERTS = 32
TOP_K = 4
SWIGLU_LIMIT = 7.0
SWIGLU_ALPHA = 1.702
EPS = 1e-5
ATT_SCALE = 1.0 / math.sqrt(HEAD_DIM)
BLOCK = 128

LANES = 128
SUBLANES = 8
ROW_TILES = D_MODEL // LANES
HALF_GROUPS = N_GROUPS // 2
HALF_IN = HALF_GROUPS * SSM_GROUP
HALF_STATE = HALF_GROUPS * STATE_DIM
VMEM_LIMIT = 56 * 1024 * 1024

PROJ_T = 32
S5_T = 32
MERGE_T = SUBLANES
PAGES_PER_STEP = 16
EXPERT_TILE = 512


def _rms(x, g):
    return x * lax.rsqrt(jnp.mean(x * x, axis=-1, keepdims=True) + EPS) * g


def _dot(a, b, **kw):
    return jnp.dot(a, b, preferred_element_type=F32, **kw)


def _dot_nt(a, b):
    return lax.dot_general(a, b, (((1,), (1,)), ((), ())), preferred_element_type=F32)


def _params(sem=None):
    return pltpu.CompilerParams(dimension_semantics=sem, vmem_limit_bytes=VMEM_LIMIT)


def _tile_rows(ref, r, n=1):
    return ref.at[pl.ds(pl.multiple_of(r * ROW_TILES, ROW_TILES), n * ROW_TILES)]


def _load_rows(ref, n):
    return jnp.concatenate([ref[pl.ds(c, n, stride=ROW_TILES), :] for c in range(ROW_TILES)], axis=1)


def _store_rows(ref, val, n):
    for c in range(ROW_TILES):
        ref[pl.ds(c, n, stride=ROW_TILES), :] = val[:, c * LANES:(c + 1) * LANES]


def _proj_body(x_ref, g_ref, w_ref, q_ref, k_ref, v_ref, kb_ref, vb_ref, u_ref, *, nb, tt):
    x = x_ref[...].reshape(nb * tt, D_MODEL)
    h = _rms(x, g_ref[...]).astype(BF16)
    p = _dot(h, w_ref[...])
    q = p[:, :ATT_WIDTH] * ATT_SCALE
    k = p[:, ATT_WIDTH:2 * ATT_WIDTH]
    v = p[:, 2 * ATT_WIDTH:3 * ATT_WIDTH]
    u = p[:, 3 * ATT_WIDTH:]
    q_ref[...] = q.astype(BF16).reshape(nb, tt, ATT_WIDTH)
    k_ref[...] = k.reshape(nb, tt, ATT_WIDTH)
    v_ref[...] = v.reshape(nb, tt, ATT_WIDTH)
    kb_ref[...] = k.astype(BF16).reshape(nb, tt, ATT_WIDTH)
    vb_ref[...] = v.astype(BF16).reshape(nb, tt, ATT_WIDTH)
    for b in range(nb):
        u_ref[:, b, :] = u[b * tt:(b + 1) * tt, :]


def _project(x, g, w, tt):
    nb, t, _ = x.shape
    spec_x = pl.BlockSpec((nb, tt, D_MODEL), lambda i: (0, i, 0))
    spec_a = pl.BlockSpec((nb, tt, ATT_WIDTH), lambda i: (0, i, 0))
    shp = lambda dt: jax.ShapeDtypeStruct((nb, t, ATT_WIDTH), dt)
    return pl.pallas_call(
        functools.partial(_proj_body, nb=nb, tt=tt),
        grid=(t // tt,),
        in_specs=[spec_x,
                  pl.BlockSpec((1, D_MODEL), lambda i: (0, 0)),
                  pl.BlockSpec(w.shape, lambda i: (0, 0))],
        out_specs=[spec_a, spec_a, spec_a, spec_a, spec_a,
                   pl.BlockSpec((tt, nb, SSM_WIDTH), lambda i: (i, 0, 0))],
        out_shape=[shp(BF16), shp(F32), shp(F32), shp(BF16), shp(BF16),
                   jax.ShapeDtypeStruct((t, nb, SSM_WIDTH), F32)],
        compiler_params=_params(("arbitrary",)),
        name="proj",
    )(x, g, w)


def _softplus(z):
    return jnp.maximum(z, 0.0) + jnp.log(1.0 + jnp.exp(-jnp.abs(z)))


def _sb_split(z, valid):
    sp = _softplus(z)
    if valid is not None:
        sp = jnp.where(valid, sp, 0.0)
    hi = sp.astype(BF16)
    lo = (sp - hi.astype(F32)).astype(BF16)
    return jnp.concatenate([hi, lo], axis=1)


def _sb_weights(z, ext, valid, carry):
    a = jnp.exp(z - ext[:, :BLOCK] - carry)
    if valid is not None:
        a = jnp.where(valid, a, 0.0)
    return a.astype(BF16), carry + ext[:, BLOCK:]


def _suffix_matrix():
    j = lax.broadcasted_iota(I32, (2 * BLOCK, 2 * BLOCK), 0)
    s = lax.broadcasted_iota(I32, (2 * BLOCK, 2 * BLOCK), 1)
    j = jnp.where(j >= BLOCK, j - BLOCK, j)
    return ((s >= BLOCK) | (j >= s)).astype(BF16)


def _attn_prompt_body(bias_ref, q_ref, k_ref, v_ref, km_ref, vm_ref, o_ref, qm_ref, carry_ref, acc_ref,
                      *, n_meta):
    jq = pl.program_id(1)
    m2 = _suffix_matrix()
    row = lax.broadcasted_iota(I32, (BLOCK, BLOCK), 0)
    col = lax.broadcasted_iota(I32, (BLOCK, LANES), 1)
    causal = col < row
    meta_valid = col >= BLOCK - n_meta
    low_half = col < HEAD_DIM
    pair_lanes = lambda p: slice(p * LANES, (p + 1) * LANES)

    for h in range(N_HEADS):
        q_pair = q_ref[0, :, pair_lanes(h // 2)]
        qm_ref[h] = jnp.where(low_half == (h % 2 == 0), q_pair, jnp.zeros_like(q_pair))
    carry_ref[...] = jnp.zeros_like(carry_ref)
    acc_ref[...] = jnp.zeros_like(acc_ref)

    def blocks(sources, valid):
        heads = range(N_HEADS)
        zs = [[_dot_nt(qm_ref[h], k_of(pair_lanes(h // 2))) + bias_ref[h] for h in heads]
              for k_of, _ in sources]
        exts = [[_dot(_sb_split(z, valid), m2) for z in zb] for zb in zs]
        pv = []
        for h in heads:
            carry, out = carry_ref[h], None
            for s, (_, v_of) in enumerate(sources):
                a, carry = _sb_weights(zs[s][h], exts[s][h], valid, carry)
                d = _dot(a, v_of(pair_lanes(h // 2)))
                out = d if out is None else out + d
            carry_ref[h] = carry
            pv.append(out)
        for p in range(N_HEADS // 2):
            acc_ref[:, pair_lanes(p)] += jnp.where(low_half, pv[2 * p], pv[2 * p + 1])

    def key_block(j):
        rows = pl.ds(pl.multiple_of(j * BLOCK, BLOCK), BLOCK)
        return (lambda ls: k_ref[0, rows, ls]), (lambda ls: v_ref[0, rows, ls])

    blocks([key_block(jq)], causal)

    def older_pair(i, c):
        blocks([key_block(jq - 1 - 2 * i), key_block(jq - 2 - 2 * i)], None)
        return c

    lax.fori_loop(0, jq // 2, older_pair, 0)

    @pl.when(jq % 2 == 1)
    def _():
        blocks([key_block(0)], None)

    blocks([((lambda ls: km_ref[:, ls]), (lambda ls: vm_ref[:, ls]))], meta_valid)
    o_ref[0] = acc_ref[...]


def _attn_prompt(bias, q, k, v, km, vm, n_meta):
    b, t, _ = q.shape
    return pl.pallas_call(
        functools.partial(_attn_prompt_body, n_meta=n_meta),
        grid=(b, t // BLOCK),
        in_specs=[pl.BlockSpec(memory_space=pltpu.SMEM),
                  pl.BlockSpec((1, BLOCK, ATT_WIDTH), lambda i, j: (i, j, 0)),
                  pl.BlockSpec((1, t, ATT_WIDTH), lambda i, j: (i, 0, 0)),
                  pl.BlockSpec((1, t, ATT_WIDTH), lambda i, j: (i, 0, 0)),
                  pl.BlockSpec((BLOCK, ATT_WIDTH), lambda i, j: (0, 0)),
                  pl.BlockSpec((BLOCK, ATT_WIDTH), lambda i, j: (0, 0))],
        out_specs=pl.BlockSpec((1, BLOCK, ATT_WIDTH), lambda i, j: (i, j, 0)),
        out_shape=jax.ShapeDtypeStruct((b, t, ATT_WIDTH), F32),
        scratch_shapes=[pltpu.VMEM((N_HEADS, BLOCK, LANES), BF16),
                        pltpu.VMEM((N_HEADS, BLOCK, LANES), F32),
                        pltpu.VMEM((BLOCK, ATT_WIDTH), F32)],
        compiler_params=_params(("arbitrary", "arbitrary")),
        name="attn_prompt",
    )(bias, q, k, v, km, vm)


def _attn_sample_body(pt_ref, q_ref, bias_ref, k_hbm, v_hbm, o_ref, carry_ref, acc_ref, kbuf, vbuf, sem,
                      *, pps, n_pages):
    b, c = pl.program_id(0), pl.program_id(1)
    n_chunks = pl.num_programs(1)
    step = b * n_chunks + c
    slot = step % 2

    def fetch(step, slot):
        sb, sc = step // n_chunks, step % n_chunks
        base = sb * n_pages + (n_chunks - 1 - sc) * pps
        for i in range(pps):
            page = pt_ref[base + i]
            pltpu.make_async_copy(k_hbm.at[page], kbuf.at[slot, i], sem.at[0, slot]).start()
            pltpu.make_async_copy(v_hbm.at[page], vbuf.at[slot, i], sem.at[1, slot]).start()

    @pl.when(step == 0)
    def _():
        fetch(step, slot)

    @pl.when(step + 1 < pl.num_programs(0) * n_chunks)
    def _():
        fetch(step + 1, 1 - slot)

    @pl.when(c == 0)
    def _():
        carry_ref[...] = jnp.zeros_like(carry_ref)
        acc_ref[...] = jnp.zeros_like(acc_ref)

    pltpu.make_async_copy(k_hbm.at[pl.ds(0, pps)], kbuf.at[slot], sem.at[0, slot]).wait()
    pltpu.make_async_copy(v_hbm.at[pl.ds(0, pps)], vbuf.at[slot], sem.at[1, slot]).wait()

    q = q_ref[0]
    order = list(reversed(range(pps)))
    z = jnp.concatenate([_dot(q, kbuf[slot, i].astype(BF16)) + bias_ref[...] for i in order], axis=0)
    ext = _dot(_sb_split(z, None), _suffix_matrix())
    carry = carry_ref[...]
    carries = []
    for j in range(pps):
        carries.append(carry)
        carry = carry + ext[j * N_HEADS:(j + 1) * N_HEADS, BLOCK:]
    a = jnp.exp(z - ext[:, :BLOCK] - jnp.concatenate(carries, axis=0))
    acc = acc_ref[...]
    for j, i in enumerate(order):
        a_j = a[j * N_HEADS:(j + 1) * N_HEADS].astype(BF16)
        acc = acc + _dot_nt(a_j, vbuf[slot, i].astype(BF16))
    carry_ref[...] = carry
    acc_ref[...] = acc

    @pl.when(c == n_chunks - 1)
    def _():
        head = lax.broadcasted_iota(I32, acc.shape, 0)
        lane = lax.broadcasted_iota(I32, acc.shape, 1)
        own = lane // HEAD_DIM == head
        o_ref[0] = jnp.sum(jnp.where(own, acc, 0.0), axis=0, keepdims=True)


def _attn_sample(page_table, q_bd, bias, cache_k, cache_v):
    db, n_pages = page_table.shape
    page = cache_k.shape[2]
    assert page == BLOCK
    pps = min(PAGES_PER_STEP, n_pages)
    n_chunks = n_pages // pps
    grid_spec = pltpu.PrefetchScalarGridSpec(
        num_scalar_prefetch=1,
        grid=(db, n_chunks),
        in_specs=[pl.BlockSpec((1, N_HEADS, ATT_WIDTH), lambda b, c, pt: (b, 0, 0)),
                  pl.BlockSpec((N_HEADS, 1), lambda b, c, pt: (0, 0)),
                  pl.BlockSpec(memory_space=pl.ANY), pl.BlockSpec(memory_space=pl.ANY)],
        out_specs=pl.BlockSpec((1, 1, ATT_WIDTH), lambda b, c, pt: (b, 0, 0)),
        scratch_shapes=[pltpu.VMEM((N_HEADS, LANES), F32), pltpu.VMEM((N_HEADS, ATT_WIDTH), F32),
                        pltpu.VMEM((2, pps, ATT_WIDTH, page), F32), pltpu.VMEM((2, pps, ATT_WIDTH, page), F32),
                        pltpu.SemaphoreType.DMA((2, 2))],
    )
    return pl.pallas_call(
        functools.partial(_attn_sample_body, pps=pps, n_pages=n_pages),
        grid_spec=grid_spec,
        out_shape=jax.ShapeDtypeStruct((db, 1, ATT_WIDTH), F32),
        compiler_params=_params(("arbitrary", "arbitrary")),
        name="attn_sample",
    )(page_table.reshape(-1), q_bd, bias, cache_k, cache_v)


def _s5_prep_body(lr_ref, li_ref, ldt_ref, br_ref, bi_ref, ar_ref, ai_ref, bbr_ref, bbi_ref):
    lr, li = lr_ref[...], li_ref[...]
    dt = jnp.exp(ldt_ref[...])
    mag = jnp.exp(lr * dt)
    ang = li * dt
    ab_re, ab_im = mag * jnp.cos(ang), mag * jnp.sin(ang)
    den = lr * lr + li * li
    num_re = ab_re - 1.0
    coef_re = (num_re * lr + ab_im * li) / den
    coef_im = (ab_im * lr - num_re * li) / den
    br, bi = br_ref[...], bi_ref[...]
    ar_ref[...] = ab_re
    ai_ref[...] = ab_im
    bbr_ref[...] = coef_re * br - coef_im * bi
    bbi_ref[...] = coef_re * bi + coef_im * br


def _s5_prep(lam_re, lam_im, log_dt, b_re, b_im):
    g, n = lam_re.shape
    v3 = lambda a: a.reshape(g, 1, n)
    bt = lambda a: jnp.swapaxes(a, 1, 2)
    small = jax.ShapeDtypeStruct((g, 1, n), F32)
    big = jax.ShapeDtypeStruct((g, SSM_GROUP, n), F32)
    return pl.pallas_call(
        _s5_prep_body, out_shape=[small, small, big, big], name="s5_prep",
    )(v3(lam_re), v3(lam_im), jnp.broadcast_to(log_dt[:, None, None], (g, 1, n)), bt(b_re), bt(b_im))


def _s5_body(u_ref, um_ref, h0_ref, bbd_ref, cre_ref, cim_ref, are_ref, aim_ref, d_ref,
             wglu_ref, bglu_ref, g_ref, s_ref, hf_ref, st_ref, bu_ref, *, tc, nb, n_meta, lane_w):
    i = pl.program_id(0)
    rows = tc * nb
    precision = lax.Precision.HIGHEST if bbd_ref.dtype == F32 else None

    def drive(uh, h):
        return _dot(uh.astype(bbd_ref.dtype), bbd_ref[h], precision=precision)

    @pl.when(i == 0)
    def _():
        st_ref[...] = h0_ref[...]
        for h in range(2 if n_meta else 0):
            bum = drive(um_ref[:, h * HALF_IN:(h + 1) * HALF_IN], h)
            ar, ai = are_ref[h], aim_ref[h]
            xr, xi = st_ref[h, 0], st_ref[h, 1]
            for t in range(n_meta):
                br, bi = bum[t:t + 1, :HALF_STATE], bum[t:t + 1, HALF_STATE:]
                xr, xi = ar * xr - ai * xi + br, ar * xi + ai * xr + bi
            st_ref[h, 0] = xr
            st_ref[h, 1] = xi

    u = u_ref[...].reshape(rows, SSM_WIDTH)
    ys = []
    for h in range(2):
        bu = drive(u[:, h * HALF_IN:(h + 1) * HALF_IN], h)
        bu_ref[0] = bu[:, :HALF_STATE]
        bu_ref[1] = bu[:, HALF_STATE:]
        for lc in range(HALF_STATE // lane_w):
            sl = slice(lc * lane_w, (lc + 1) * lane_w)
            ar = jnp.broadcast_to(are_ref[h, :, sl], (nb, lane_w))
            ai = jnp.broadcast_to(aim_ref[h, :, sl], (nb, lane_w))

            def step(t, x, sl=sl, ar=ar, ai=ai):
                xr, xi = x
                r0 = pl.multiple_of(t * nb, nb)
                nxr = ar * xr - ai * xi + bu_ref[0, pl.ds(r0, nb), sl]
                nxi = ar * xi + ai * xr + bu_ref[1, pl.ds(r0, nb), sl]
                bu_ref[0, pl.ds(r0, nb), sl] = nxr
                bu_ref[1, pl.ds(r0, nb), sl] = nxi
                return nxr, nxi

            xr, xi = lax.fori_loop(0, tc, step, (st_ref[h, 0, :, sl], st_ref[h, 1, :, sl]))
            st_ref[h, 0, :, sl] = xr
            st_ref[h, 1, :, sl] = xi
        ys.append(_dot(bu_ref[0].astype(BF16), cre_ref[h]) - _dot(bu_ref[1].astype(BF16), cim_ref[h]))
    y = jnp.concatenate(ys, axis=1) + d_ref[...] * u
    y = jax.nn.gelu(y)
    y = y * jax.nn.sigmoid(_dot(y.astype(BF16), wglu_ref[...]) + bglu_ref[...])
    s_ref[...] = _rms(y, g_ref[...]).reshape(tc, nb, SSM_WIDTH)

    @pl.when(i == pl.num_programs(0) - 1)
    def _():
        hf_ref[...] = st_ref[...]


def _s5(u, u_meta, h0, bbd, cre, cim, a_re, a_im, d, w_glu, b_glu, g):
    t, nb, _ = u.shape
    tc = min(S5_T, t)
    n_meta = 0 if u_meta is None else u_meta.shape[0]
    if u_meta is None:
        u_meta = jnp.zeros((SUBLANES, SSM_WIDTH), F32)
    lane_w = max(LANES, min(4 * LANES, (64 * LANES) // nb))
    full = lambda a: pl.BlockSpec(a.shape, lambda i: (0,) * a.ndim)
    blk = pl.BlockSpec((tc, nb, SSM_WIDTH), lambda i: (i, 0, 0))
    args = (u, u_meta, h0, bbd, cre, cim, a_re, a_im, d, w_glu, b_glu, g)
    return pl.pallas_call(
        functools.partial(_s5_body, tc=tc, nb=nb, n_meta=n_meta, lane_w=lane_w),
        grid=(t // tc,),
        in_specs=[blk] + [full(a) for a in args[1:]],
        out_specs=[blk, full(h0)],
        out_shape=[jax.ShapeDtypeStruct(u.shape, F32), jax.ShapeDtypeStruct(h0.shape, F32)],
        scratch_shapes=[pltpu.VMEM(h0.shape, F32), pltpu.VMEM((2, tc * nb, HALF_STATE), F32)],
        compiler_params=_params(("arbitrary",)),
        name="s5",
    )(*args)


def _block_diag_in(bb):
    eye = jnp.eye(HALF_GROUPS, dtype=F32)
    bb = bb.reshape(2, HALF_GROUPS, SSM_GROUP, STATE_DIM)
    return jnp.einsum('hgcn,gk->hgckn', bb, eye).reshape(2, HALF_IN, HALF_STATE)


def _block_diag_out(c):
    eye = jnp.eye(HALF_GROUPS, dtype=F32)
    c = c.reshape(2, HALF_GROUPS, SSM_GROUP, STATE_DIM)
    return jnp.einsum('hgcn,kg->hkngc', c, eye).reshape(2, HALF_STATE, HALF_IN)


def _state_in(s):
    nb = s.shape[0]
    return jnp.swapaxes(s.reshape(nb, 2, HALF_STATE), 0, 1)


def _state_out(s):
    nb = s.shape[1]
    return jnp.swapaxes(s, 0, 1).reshape(nb, N_GROUPS, STATE_DIM)


def _merge_body(xp_ref, ap_ref, sp_ref, xs_ref, as_ref, ss_ref, gatt_ref, wout_ref, gffn_ref,
                wr_ref, br_ref, h1_ref, f3_ref, idx_ref, wts_ref, rank_ref, cnt_ref, run_ref,
                *, nb, n_blk):
    i = pl.program_id(0)
    rows = nb * MERGE_T
    is_sample = i == n_blk

    @pl.when(i == 0)
    def _():
        run_ref[...] = jnp.zeros_like(run_ref)

    s_bt = jnp.concatenate([sp_ref[:, b, :] for b in range(nb)], axis=0)
    x = jnp.where(is_sample, xs_ref[...], xp_ref[...].reshape(rows, D_MODEL))
    att = jnp.where(is_sample, as_ref[...], ap_ref[...].reshape(rows, ATT_WIDTH))
    ssm = jnp.where(is_sample, ss_ref[...], s_bt)
    cat = jnp.concatenate([_rms(att, gatt_ref[...]), ssm], axis=1).astype(BF16)
    h1 = x + _dot(cat, wout_ref[...])
    f = _rms(h1, gffn_ref[...])
    h1_ref[...] = h1
    _store_rows(f3_ref, f, rows)

    logits = _dot(f, wr_ref[...], precision=lax.Precision.HIGHEST) + br_ref[...]
    lane = lax.broadcasted_iota(I32, logits.shape, 1)
    vals, ids, hots = [], [], []
    for _ in range(TOP_K):
        m = jnp.max(logits, axis=1, keepdims=True)
        sel = jnp.min(jnp.where(logits == m, lane, N_EXPERTS), axis=1, keepdims=True)
        hot = lane == sel
        vals.append(m)
        ids.append(sel)
        hots.append(hot)
        logits = jnp.where(hot, -jnp.inf, logits)
    es = [jnp.exp(v - vals[0]) for v in vals]
    den = es[0] + es[1] + es[2] + es[3]

    chosen = (hots[0] | hots[1] | hots[2] | hots[3])
    tr = lax.broadcasted_iota(I32, (rows, rows), 0)
    tc = lax.broadcasted_iota(I32, (rows, rows), 1)
    before = _dot((tc < tr).astype(BF16), chosen.astype(BF16)) + run_ref[...]
    run_ref[...] = run_ref[...] + jnp.sum(chosen.astype(F32), axis=0, keepdims=True)
    cnt_ref[...] = run_ref[...]

    out_lane = lax.broadcasted_iota(I32, (rows, LANES), 1)
    idx_o = jnp.zeros((rows, LANES), I32)
    wts_o = jnp.zeros((rows, LANES), F32)
    rank_o = jnp.zeros((rows, LANES), I32)
    for k in range(TOP_K):
        rank_k = jnp.sum(jnp.where(hots[k], before, 0.0), axis=1, keepdims=True).astype(I32)
        idx_o = jnp.where(out_lane == k, ids[k], idx_o)
        wts_o = jnp.where(out_lane == k, es[k] / den, wts_o)
        rank_o = jnp.where(out_lane == k, rank_k, rank_o)
    idx_ref[...] = idx_o
    wts_ref[...] = wts_o
    rank_ref[...] = rank_o


def _merge(x_p, att_p, s_p, x_s, att_s, s_s, g_att, w_out, g_ffn, w_r, b_r):
    nb, t, _ = x_p.shape
    rows = nb * MERGE_T
    n_blk = t // MERGE_T
    n_tok = (n_blk + 1) * rows
    pj = lambda i: jnp.minimum(i, n_blk - 1)
    full = lambda a: pl.BlockSpec(a.shape, lambda i: (0,) * a.ndim)
    row_spec = lambda w: pl.BlockSpec((rows, w), lambda i: (i, 0))
    return pl.pallas_call(
        functools.partial(_merge_body, nb=nb, n_blk=n_blk),
        grid=(n_blk + 1,),
        in_specs=[pl.BlockSpec((nb, MERGE_T, D_MODEL), lambda i: (0, pj(i), 0)),
                  pl.BlockSpec((nb, MERGE_T, ATT_WIDTH), lambda i: (0, pj(i), 0)),
                  pl.BlockSpec((MERGE_T, nb, SSM_WIDTH), lambda i: (pj(i), 0, 0)),
                  full(x_s), full(att_s), full(s_s), full(g_att), full(w_out), full(g_ffn),
                  full(w_r), full(b_r)],
        out_specs=[row_spec(D_MODEL),
                   pl.BlockSpec((rows * ROW_TILES, LANES), lambda i: (i, 0)),
                   row_spec(LANES), row_spec(LANES), row_spec(LANES),
                   pl.BlockSpec((1, N_EXPERTS), lambda i: (0, 0))],
        out_shape=[jax.ShapeDtypeStruct((n_tok, D_MODEL), F32),
                   jax.ShapeDtypeStruct((n_tok * ROW_TILES, LANES), F32),
                   jax.ShapeDtypeStruct((n_tok, LANES), I32),
                   jax.ShapeDtypeStruct((n_tok, LANES), F32),
                   jax.ShapeDtypeStruct((n_tok, LANES), I32),
                   jax.ShapeDtypeStruct((1, N_EXPERTS), F32)],
        scratch_shapes=[pltpu.VMEM((1, N_EXPERTS), F32)],
        compiler_params=_params(("arbitrary",)),
        name="merge_route",
    )(x_p, att_p, s_p, x_s, att_s, s_s, g_att, w_out, g_ffn, w_r, b_r)


def _dispatch_body(tail_ref, has_ref, nval_ref, pos_ref, f3_ref, o_hbm, zero_ref, sem,
                   *, rows, tile, n_tiles):
    i = pl.program_id(0)

    def tile_copy(start):
        return pltpu.make_async_copy(zero_ref, _tile_rows(o_hbm, start, tile), sem)

    @pl.when(i == 0)
    def _():
        zero_ref[...] = jnp.zeros_like(zero_ref)

        def start(e, c):
            @pl.when(has_ref[e] > 0)
            def _():
                tile_copy(tail_ref[e]).start()
            return c

        def wait(e, c):
            @pl.when(has_ref[e] > 0)
            def _():
                tile_copy(tail_ref[e]).wait()
            return c

        def start_unused(j, c):
            tile_copy(pl.multiple_of(j * tile, tile)).start()
            return c

        def wait_unused(j, c):
            tile_copy(pl.multiple_of(j * tile, tile)).wait()
            return c

        lax.fori_loop(0, N_EXPERTS, start, 0)
        lax.fori_loop(nval_ref[0], n_tiles, start_unused, 0)
        lax.fori_loop(0, N_EXPERTS, wait, 0)
        lax.fori_loop(nval_ref[0], n_tiles, wait_unused, 0)

    def issue(r, c):
        for k in range(TOP_K):
            pltpu.make_async_copy(_tile_rows(f3_ref, r), _tile_rows(o_hbm, pos_ref[r * TOP_K + k]),
                                  sem).start(priority=k % 2)
        return c

    lax.fori_loop(0, rows, issue, 0)
    for _ in range(TOP_K):
        pltpu.make_async_copy(f3_ref, _tile_rows(o_hbm, 0, rows), sem).wait()


def _dispatch(tail_start, has_rows, n_valid, pos, f3, rows, m_tot, tile):
    n_tok = f3.shape[0] // ROW_TILES
    grid_spec = pltpu.PrefetchScalarGridSpec(
        num_scalar_prefetch=3,
        grid=(n_tok // rows,),
        in_specs=[pl.BlockSpec((rows * TOP_K,), lambda i, *_: (i,), memory_space=pltpu.SMEM),
                  pl.BlockSpec((rows * ROW_TILES, LANES), lambda i, *_: (i, 0))],
        out_specs=pl.BlockSpec(memory_space=pl.ANY),
        scratch_shapes=[pltpu.VMEM((tile * ROW_TILES, LANES), F32), pltpu.SemaphoreType.DMA(())],
    )
    return pl.pallas_call(
        functools.partial(_dispatch_body, rows=rows, tile=tile, n_tiles=m_tot // tile),
        grid_spec=grid_spec,
        out_shape=jax.ShapeDtypeStruct((m_tot * ROW_TILES, LANES), F32),
        compiler_params=_params(("arbitrary",)),
        name="dispatch",
    )(tail_start, has_rows, n_valid, pos, f3)


def _experts_body(tidx_ref, te_ref, nval_ref, x_ref, wg_ref, wu_ref, wd_ref, bg_ref, bu_ref, bd_ref,
                  y_ref, wg_s, wu_s, wd_s):
    i = pl.program_id(0)
    e = te_ref[i]
    prev = te_ref[jnp.maximum(i - 1, 0)]
    tile = x_ref.shape[0] // ROW_TILES

    @pl.when((i == 0) | (e != prev))
    def _():
        wg_s[...] = wg_ref[0].astype(BF16)
        wu_s[...] = wu_ref[0].astype(BF16)
        wd_s[...] = wd_ref[0].astype(BF16)

    @pl.when(i < nval_ref[0])
    def _():
        x = _load_rows(x_ref, tile).astype(BF16)
        g = jnp.minimum(_dot(x, wg_s[...]) + bg_ref[0], SWIGLU_LIMIT)
        up = jnp.clip(_dot(x, wu_s[...]) + bu_ref[0], -SWIGLU_LIMIT, SWIGLU_LIMIT)
        act = g * jax.nn.sigmoid(SWIGLU_ALPHA * g) * (up + 1.0)
        y = _dot(act.astype(BF16), wd_s[...]) + bd_ref[0]
        _store_rows(y_ref, y, tile)

    @pl.when(i >= nval_ref[0])
    def _():
        y_ref[...] = jnp.zeros_like(y_ref)


def _experts(tile_idx, tile_e, n_valid, xs, w_gate, w_up, w_down, b_gate, b_up, b_down, tile):
    n_tiles = tile_idx.shape[0]
    d_ff = w_gate.shape[-1]
    row = pl.BlockSpec((tile * ROW_TILES, LANES), lambda i, ti, te, nv: (ti[i], 0))
    w_spec = lambda a: pl.BlockSpec((1,) + a.shape[1:], lambda i, ti, te, nv: (te[i], 0, 0))
    b3 = lambda b: b.reshape(N_EXPERTS, 1, b.shape[-1])
    grid_spec = pltpu.PrefetchScalarGridSpec(
        num_scalar_prefetch=3,
        grid=(n_tiles,),
        in_specs=[row, w_spec(w_gate), w_spec(w_up), w_spec(w_down),
                  w_spec(b3(b_gate)), w_spec(b3(b_up)), w_spec(b3(b_down))],
        out_specs=pl.BlockSpec((tile * ROW_TILES, LANES), lambda i, ti, te, nv: (i, 0)),
        scratch_shapes=[pltpu.VMEM((D_MODEL, d_ff), BF16), pltpu.VMEM((D_MODEL, d_ff), BF16),
                        pltpu.VMEM((d_ff, D_MODEL), BF16)],
    )
    return pl.pallas_call(
        _experts_body,
        grid_spec=grid_spec,
        out_shape=jax.ShapeDtypeStruct(xs.shape, F32),
        compiler_params=_params(("arbitrary",)),
        name="experts",
    )(tile_idx, tile_e, n_valid, xs, w_gate, w_up, w_down, b3(b_gate), b3(b_up), b3(b_down))


def _combine_body(pos_ref, next_pos_ref, wts_ref, h1_ref, g_ref, y_hbm, yp_ref, ys_ref, *scratch,
                  nb, n_blk, rows):
    bufs, sem = (scratch[:TOP_K], scratch[TOP_K:2 * TOP_K]), scratch[2 * TOP_K]
    i = pl.program_id(0)

    def gather(p_ref, slot):
        def issue(r, c):
            for k in range(TOP_K):
                pltpu.make_async_copy(_tile_rows(y_hbm, p_ref[r * TOP_K + k]),
                                      _tile_rows(bufs[slot][k], r), sem.at[slot]).start(priority=k % 2)
            return c
        lax.fori_loop(0, rows, issue, 0)

    def finish(slot):
        for k in range(TOP_K):
            pltpu.make_async_copy(_tile_rows(y_hbm, 0, rows), bufs[slot][k], sem.at[slot]).wait()
        w = wts_ref[...]
        moe = sum(w[:, k:k + 1] * _load_rows(bufs[slot][k], rows) for k in range(TOP_K))
        y = _rms(h1_ref[...] + moe, g_ref[...])

        @pl.when(i < n_blk)
        def _():
            yp_ref[...] = y.reshape(nb, MERGE_T, D_MODEL)

        @pl.when(i == n_blk)
        def _():
            ys_ref[...] = y

    @pl.when(i == 0)
    def _():
        gather(pos_ref, 0)

    for slot in range(2):
        @pl.when(i % 2 == slot)
        def _(slot=slot):
            @pl.when(i < n_blk)
            def _():
                gather(next_pos_ref, 1 - slot)
            finish(slot)


def _combine(pos, wts, h1, g, ys, nb, t):
    rows = nb * MERGE_T
    n_blk = t // MERGE_T
    return pl.pallas_call(
        functools.partial(_combine_body, nb=nb, n_blk=n_blk, rows=rows),
        grid=(n_blk + 1,),
        in_specs=[pl.BlockSpec((rows * TOP_K,), lambda i: (i,), memory_space=pltpu.SMEM),
                  pl.BlockSpec((rows * TOP_K,), lambda i: (jnp.minimum(i + 1, n_blk),),
                               memory_space=pltpu.SMEM),
                  pl.BlockSpec((rows, LANES), lambda i: (i, 0)),
                  pl.BlockSpec((rows, D_MODEL), lambda i: (i, 0)),
                  pl.BlockSpec((1, D_MODEL), lambda i: (0, 0)),
                  pl.BlockSpec(memory_space=pl.ANY)],
        out_specs=[pl.BlockSpec((nb, MERGE_T, D_MODEL), lambda i: (0, jnp.minimum(i, n_blk - 1), 0)),
                   pl.BlockSpec((rows, D_MODEL), lambda i: (0, 0))],
        out_shape=[jax.ShapeDtypeStruct((nb, t, D_MODEL), F32),
                   jax.ShapeDtypeStruct((rows, D_MODEL), F32)],
        scratch_shapes=[pltpu.VMEM((rows * ROW_TILES, LANES), F32)] * (2 * TOP_K)
                       + [pltpu.SemaphoreType.DMA((2,))],
        compiler_params=_params(("arbitrary",)),
        name="combine",
    )(pos, pos, wts, h1, g, ys)


def _moe_plan(counts, idx, rank, tile):
    n_tok = idx.shape[0]
    n_tiles = (n_tok * TOP_K) // tile + N_EXPERTS
    padded = ((counts + tile - 1) // tile) * tile
    ends = jnp.cumsum(padded)
    off = ends - padded
    pos = (off[idx] + rank).reshape(-1)
    tile_ends = ends // tile
    n_valid = tile_ends[-1]
    tile_idx = jnp.minimum(jnp.arange(n_tiles, dtype=I32), n_valid - 1)
    tile_e = jnp.minimum(jnp.sum(tile_idx[:, None] >= tile_ends[None, :], axis=1), N_EXPERTS - 1).astype(I32)
    return pos.astype(I32), tile_idx, tile_e, n_valid.reshape(1).astype(I32), (ends - tile).astype(I32), \
        (counts > 0).astype(I32), n_tiles * tile


def kernel(x_prompt, x_sample, cache_k, cache_v, page_table, state_ssm_re, state_ssm_im, meta_tokens, g_mix, w_in, sb_bias, lambda_re, lambda_im, log_dt, b_re, b_im, c_re, c_im, d_skip, w_glu, b_glu, g_att_out, g_ssm_out, w_out, g_ffn, w_router, b_router, w_gate, b_gate, w_up, b_up, w_down, b_down, g_final):
    depth = g_mix.shape[0]
    assert depth == 1, "single-layer step"
    b, t, _ = x_prompt.shape
    db, ds, _ = x_sample.shape
    n_meta = meta_tokens.shape[0]
    assert ds == 1 and db == b * MERGE_T and t % BLOCK == 0 and n_meta <= BLOCK
    l = 0
    row = lambda a: a.reshape(1, -1)

    w_in_b = w_in[l].astype(BF16)
    g_mix_l = row(g_mix[l])
    q_p, k_p, v_p, kb_p, vb_p, u_p = _project(x_prompt, g_mix_l, w_in_b, PROJ_T)
    _, k_m, v_m, kb_m, vb_m, u_m = _project(meta_tokens[None], g_mix_l, w_in_b, n_meta)
    q_s, k_s, v_s, _, _, u_s = _project(x_sample.reshape(1, db, D_MODEL), g_mix_l, w_in_b, db)

    pad_meta = lambda a: jnp.pad(a[0], ((BLOCK - n_meta, 0), (0, 0)))
    att_p = _attn_prompt(sb_bias[l], q_p, kb_p, vb_p, pad_meta(kb_m), pad_meta(vb_m), n_meta)
    n_phys, page = cache_k.shape[1], cache_k.shape[2]
    head_of_lane = jnp.arange(ATT_WIDTH) // HEAD_DIM
    q_bd = jnp.where(head_of_lane[None, None, :] == jnp.arange(N_HEADS)[None, :, None],
                     q_s[0][:, None, :], jnp.zeros((), BF16))
    keys_minor = lambda c: jnp.transpose(c, (0, 2, 3, 1)).reshape(n_phys, ATT_WIDTH, page)
    att_s = _attn_sample(page_table, q_bd, sb_bias[l].reshape(N_HEADS, 1),
                         keys_minor(cache_k[l]), keys_minor(cache_v[l]))

    ab_re, ab_im, bb_re, bb_im = _s5_prep(lambda_re[l], lambda_im[l], log_dt[l], b_re[l], b_im[l])
    bbd = jnp.concatenate([_block_diag_in(bb_re), _block_diag_in(bb_im)], axis=2)
    cre = _block_diag_out(c_re[l]).astype(BF16)
    cim = _block_diag_out(c_im[l]).astype(BF16)
    a_re = ab_re.reshape(2, 1, HALF_STATE)
    a_im = ab_im.reshape(2, 1, HALF_STATE)
    s5_w = (cre, cim, a_re, a_im, row(d_skip[l]), w_glu[l].astype(BF16), row(b_glu[l]), row(g_ssm_out[l]))
    zero_state = jnp.zeros((2, 2, b, HALF_STATE), F32)
    s_p, hf_p = _s5(u_p, u_m.reshape(n_meta, SSM_WIDTH), zero_state, bbd.astype(BF16), *s5_w)
    h0_s = jnp.stack([_state_in(state_ssm_re[l]), _state_in(state_ssm_im[l])], axis=1)
    s_s, hf_s = _s5(u_s.reshape(1, db, SSM_WIDTH), None, h0_s, bbd, *s5_w)

    h1, f3, idx, wts, rank, cnt = _merge(
        x_prompt, att_p, s_p, x_sample.reshape(db, D_MODEL), att_s.reshape(db, ATT_WIDTH),
        s_s.reshape(db, SSM_WIDTH), row(g_att_out[l]), w_out[l].astype(BF16), row(g_ffn[l]),
        w_router[l], row(b_router[l]))
    counts = cnt[0].astype(I32)
    pos, tile_idx, tile_e, n_valid, tail_start, has_rows, m_tot = _moe_plan(
        counts, idx[:, :TOP_K], rank[:, :TOP_K], EXPERT_TILE)
    rows = b * MERGE_T
    xs = _dispatch(tail_start, has_rows, n_valid, pos, f3, rows, m_tot, EXPERT_TILE)
    ys = _experts(tile_idx, tile_e, n_valid, xs, w_gate[l], w_up[l], w_down[l],
                  b_gate[l], b_up[l], b_down[l], EXPERT_TILE)
    y_prompt, y_sample = _combine(pos, wts, h1, row(g_final), ys, b, t)

    heads = lambda a: a.reshape(a.shape[:-1] + (N_HEADS, HEAD_DIM))
    with_meta = lambda m, p: heads(jnp.concatenate([jnp.broadcast_to(m, (b,) + m.shape[1:]), p], axis=1))[None]
    return (y_prompt, y_sample.reshape(db, ds, D_MODEL),
            with_meta(k_m, k_p), with_meta(v_m, v_p),
            heads(k_s.reshape(db, ds, ATT_WIDTH))[None], heads(v_s.reshape(db, ds, ATT_WIDTH))[None],
            _state_out(hf_p[:, 0])[None], _state_out(hf_p[:, 1])[None],
            _state_out(hf_s[:, 0])[None], _state_out(hf_s[:, 1])[None])
```

```python
import functools
import math

import jax
import jax.numpy as jnp
from jax import lax
from jax.experimental import pallas as pl
from jax.experimental.pallas import tpu as pltpu

F32 = jnp.float32
BF16 = jnp.bfloat16
I32 = jnp.int32

D_MODEL = 1024
ATT_WIDTH = 512
N_HEADS = 8
HEAD_DIM = 64
SSM_WIDTH = 512
SSM_GROUP = 16
N_GROUPS = 32
STATE_DIM = 64
N_EXPERTS = 32
TOP_K = 4
SWIGLU_LIMIT = 7.0
SWIGLU_ALPHA = 1.702
EPS = 1e-5
ATT_SCALE = 1.0 / math.sqrt(HEAD_DIM)
BLOCK = 128

LANES = 128
SUBLANES = 8
ROW_TILES = D_MODEL // LANES
HALF_GROUPS = N_GROUPS // 2
HALF_IN = HALF_GROUPS * SSM_GROUP
HALF_STATE = HALF_GROUPS * STATE_DIM
VMEM_LIMIT = 56 * 1024 * 1024

PROJ_T = 32
S5_T = 32
MERGE_T = SUBLANES
PAGES_PER_STEP = 16
EXPERT_TILE = 512
ISSUE_UNROLL = 4


def _rms(x, g):
    return x * lax.rsqrt(jnp.mean(x * x, axis=-1, keepdims=True) + EPS) * g


def _dot(a, b, **kw):
    return jnp.dot(a, b, preferred_element_type=F32, **kw)


def _dot_nt(a, b):
    return lax.dot_general(a, b, (((1,), (1,)), ((), ())), preferred_element_type=F32)


def _params(sem=None):
    return pltpu.CompilerParams(dimension_semantics=sem, vmem_limit_bytes=VMEM_LIMIT)


def _tile_rows(ref, r, n=1):
    return ref.at[pl.ds(pl.multiple_of(r * ROW_TILES, ROW_TILES), n * ROW_TILES)]


def _load_rows(ref, n):
    return jnp.concatenate([ref[pl.ds(c, n, stride=ROW_TILES), :] for c in range(ROW_TILES)], axis=1)


def _store_rows(ref, val, n):
    for c in range(ROW_TILES):
        ref[pl.ds(c, n, stride=ROW_TILES), :] = val[:, c * LANES:(c + 1) * LANES]


def _proj_body(x_ref, g_ref, w_ref, q_ref, k_ref, v_ref, kb_ref, vb_ref, u_ref, *, nb, tt):
    x = x_ref[...].reshape(nb * tt, D_MODEL)
    h = _rms(x, g_ref[...]).astype(BF16)
    p = _dot(h, w_ref[...])
    q = p[:, :ATT_WIDTH] * ATT_SCALE
    k = p[:, ATT_WIDTH:2 * ATT_WIDTH]
    v = p[:, 2 * ATT_WIDTH:3 * ATT_WIDTH]
    u = p[:, 3 * ATT_WIDTH:]
    q_ref[...] = q.astype(BF16).reshape(nb, tt, ATT_WIDTH)
    k_ref[...] = k.reshape(nb, tt, ATT_WIDTH)
    v_ref[...] = v.reshape(nb, tt, ATT_WIDTH)
    kb_ref[...] = k.astype(BF16).reshape(nb, tt, ATT_WIDTH)
    vb_ref[...] = v.astype(BF16).reshape(nb, tt, ATT_WIDTH)
    for b in range(nb):
        u_ref[:, b, :] = u[b * tt:(b + 1) * tt, :]


def _project(x, g, w, tt):
    nb, t, _ = x.shape
    spec_x = pl.BlockSpec((nb, tt, D_MODEL), lambda i: (0, i, 0))
    spec_a = pl.BlockSpec((nb, tt, ATT_WIDTH), lambda i: (0, i, 0))
    shp = lambda dt: jax.ShapeDtypeStruct((nb, t, ATT_WIDTH), dt)
    return pl.pallas_call(
        functools.partial(_proj_body, nb=nb, tt=tt),
        grid=(t // tt,),
        in_specs=[spec_x,
                  pl.BlockSpec((1, D_MODEL), lambda i: (0, 0)),
                  pl.BlockSpec(w.shape, lambda i: (0, 0))],
        out_specs=[spec_a, spec_a, spec_a, spec_a, spec_a,
                   pl.BlockSpec((tt, nb, SSM_WIDTH), lambda i: (i, 0, 0))],
        out_shape=[shp(BF16), shp(F32), shp(F32), shp(BF16), shp(BF16),
                   jax.ShapeDtypeStruct((t, nb, SSM_WIDTH), F32)],
        compiler_params=_params(("arbitrary",)),
        name="proj",
    )(x, g, w)


def _softplus(z):
    return jnp.maximum(z, 0.0) + jnp.log(1.0 + jnp.exp(-jnp.abs(z)))


def _sb_split(z, valid):
    sp = _softplus(z)
    if valid is not None:
        sp = jnp.where(valid, sp, 0.0)
    hi = sp.astype(BF16)
    lo = (sp - hi.astype(F32)).astype(BF16)
    return jnp.concatenate([hi, lo], axis=1)


def _sb_weights(z, ext, valid, carry):
    a = jnp.exp(z - ext[:, :BLOCK] - carry)
    if valid is not None:
        a = jnp.where(valid, a, 0.0)
    return a.astype(BF16), carry + ext[:, BLOCK:]


def _suffix_matrix():
    j = lax.broadcasted_iota(I32, (2 * BLOCK, 2 * BLOCK), 0)
    s = lax.broadcasted_iota(I32, (2 * BLOCK, 2 * BLOCK), 1)
    j = jnp.where(j >= BLOCK, j - BLOCK, j)
    return ((s >= BLOCK) | (j >= s)).astype(BF16)


def _attn_prompt_body(bias_ref, q_ref, k_ref, v_ref, km_ref, vm_ref, o_ref, qm_ref, carry_ref, acc_ref,
                      *, n_meta):
    jq = pl.program_id(1)
    m2 = _suffix_matrix()
    row = lax.broadcasted_iota(I32, (BLOCK, BLOCK), 0)
    col = lax.broadcasted_iota(I32, (BLOCK, LANES), 1)
    causal = col < row
    meta_valid = col >= BLOCK - n_meta
    low_half = col < HEAD_DIM
    pair_lanes = lambda p: slice(p * LANES, (p + 1) * LANES)

    for h in range(N_HEADS):
        q_pair = q_ref[0, :, pair_lanes(h // 2)]
        qm_ref[h] = jnp.where(low_half == (h % 2 == 0), q_pair, jnp.zeros_like(q_pair))
    carry_ref[...] = jnp.zeros_like(carry_ref)
    acc_ref[...] = jnp.zeros_like(acc_ref)

    def blocks(sources, valid):
        heads = range(N_HEADS)
        zs = [[_dot_nt(qm_ref[h], k_of(pair_lanes(h // 2))) + bias_ref[h] for h in heads]
              for k_of, _ in sources]
        exts = [[_dot(_sb_split(z, valid), m2) for z in zb] for zb in zs]
        pv = []
        for h in heads:
            carry, out = carry_ref[h], None
            for s, (_, v_of) in enumerate(sources):
                a, carry = _sb_weights(zs[s][h], exts[s][h], valid, carry)
                d = _dot(a, v_of(pair_lanes(h // 2)))
                out = d if out is None else out + d
            carry_ref[h] = carry
            pv.append(out)
        for p in range(N_HEADS // 2):
            acc_ref[:, pair_lanes(p)] += jnp.where(low_half, pv[2 * p], pv[2 * p + 1])

    def key_block(j):
        rows = pl.ds(pl.multiple_of(j * BLOCK, BLOCK), BLOCK)
        return (lambda ls: k_ref[0, rows, ls]), (lambda ls: v_ref[0, rows, ls])

    blocks([key_block(jq)], causal)

    def older_pair(i, c):
        blocks([key_block(jq - 1 - 2 * i), key_block(jq - 2 - 2 * i)], None)
        return c

    lax.fori_loop(0, jq // 2, older_pair, 0)

    @pl.when(jq % 2 == 1)
    def _():
        blocks([key_block(0)], None)

    blocks([((lambda ls: km_ref[:, ls]), (lambda ls: vm_ref[:, ls]))], meta_valid)
    o_ref[0] = acc_ref[...]


def _attn_prompt(bias, q, k, v, km, vm, n_meta):
    b, t, _ = q.shape
    return pl.pallas_call(
        functools.partial(_attn_prompt_body, n_meta=n_meta),
        grid=(b, t // BLOCK),
        in_specs=[pl.BlockSpec(memory_space=pltpu.SMEM),
                  pl.BlockSpec((1, BLOCK, ATT_WIDTH), lambda i, j: (i, j, 0)),
                  pl.BlockSpec((1, t, ATT_WIDTH), lambda i, j: (i, 0, 0)),
                  pl.BlockSpec((1, t, ATT_WIDTH), lambda i, j: (i, 0, 0)),
                  pl.BlockSpec((BLOCK, ATT_WIDTH), lambda i, j: (0, 0)),
                  pl.BlockSpec((BLOCK, ATT_WIDTH), lambda i, j: (0, 0))],
        out_specs=pl.BlockSpec((1, BLOCK, ATT_WIDTH), lambda i, j: (i, j, 0)),
        out_shape=jax.ShapeDtypeStruct((b, t, ATT_WIDTH), F32),
        scratch_shapes=[pltpu.VMEM((N_HEADS, BLOCK, LANES), BF16),
                        pltpu.VMEM((N_HEADS, BLOCK, LANES), F32),
                        pltpu.VMEM((BLOCK, ATT_WIDTH), F32)],
        compiler_params=_params(("arbitrary", "arbitrary")),
        name="attn_prompt",
    )(bias, q, k, v, km, vm)


def _attn_sample_body(pt_ref, q_ref, bias_ref, k_hbm, v_hbm, o_ref, carry_ref, acc_ref, kbuf, vbuf, sem,
                      *, pps, n_pages):
    b, c = pl.program_id(0), pl.program_id(1)
    n_chunks = pl.num_programs(1)
    step = b * n_chunks + c
    slot = step % 2

    def fetch(step, slot):
        sb, sc = step // n_chunks, step % n_chunks
        base = sb * n_pages + (n_chunks - 1 - sc) * pps
        for i in range(pps):
            page = pt_ref[base + i]
            pltpu.make_async_copy(k_hbm.at[page], kbuf.at[slot, i], sem.at[0, slot]).start()
            pltpu.make_async_copy(v_hbm.at[page], vbuf.at[slot, i], sem.at[1, slot]).start()

    @pl.when(step == 0)
    def _():
        fetch(step, slot)

    @pl.when(step + 1 < pl.num_programs(0) * n_chunks)
    def _():
        fetch(step + 1, 1 - slot)

    @pl.when(c == 0)
    def _():
        carry_ref[...] = jnp.zeros_like(carry_ref)
        acc_ref[...] = jnp.zeros_like(acc_ref)

    pltpu.make_async_copy(k_hbm.at[pl.ds(0, pps)], kbuf.at[slot], sem.at[0, slot]).wait()
    pltpu.make_async_copy(v_hbm.at[pl.ds(0, pps)], vbuf.at[slot], sem.at[1, slot]).wait()

    q = q_ref[0]
    order = list(reversed(range(pps)))
    z = jnp.concatenate([_dot(q, kbuf[slot, i].astype(BF16)) + bias_ref[...] for i in order], axis=0)
    ext = _dot(_sb_split(z, None), _suffix_matrix())
    carry = carry_ref[...]
    carries = []
    for j in range(pps):
        carries.append(carry)
        carry = carry + ext[j * N_HEADS:(j + 1) * N_HEADS, BLOCK:]
    a = jnp.exp(z - ext[:, :BLOCK] - jnp.concatenate(carries, axis=0))
    acc = acc_ref[...]
    for j, i in enumerate(order):
        a_j = a[j * N_HEADS:(j + 1) * N_HEADS].astype(BF16)
        acc = acc + _dot_nt(a_j, vbuf[slot, i].astype(BF16))
    carry_ref[...] = carry
    acc_ref[...] = acc

    @pl.when(c == n_chunks - 1)
    def _():
        head = lax.broadcasted_iota(I32, acc.shape, 0)
        lane = lax.broadcasted_iota(I32, acc.shape, 1)
        own = lane // HEAD_DIM == head
        o_ref[0] = jnp.sum(jnp.where(own, acc, 0.0), axis=0, keepdims=True)


def _attn_sample(page_table, q_bd, bias, cache_k, cache_v):
    db, n_pages = page_table.shape
    page = cache_k.shape[2]
    assert page == BLOCK
    pps = min(PAGES_PER_STEP, n_pages)
    n_chunks = n_pages // pps
    grid_spec = pltpu.PrefetchScalarGridSpec(
        num_scalar_prefetch=1,
        grid=(db, n_chunks),
        in_specs=[pl.BlockSpec((1, N_HEADS, ATT_WIDTH), lambda b, c, pt: (b, 0, 0)),
                  pl.BlockSpec((N_HEADS, 1), lambda b, c, pt: (0, 0)),
                  pl.BlockSpec(memory_space=pl.ANY), pl.BlockSpec(memory_space=pl.ANY)],
        out_specs=pl.BlockSpec((1, 1, ATT_WIDTH), lambda b, c, pt: (b, 0, 0)),
        scratch_shapes=[pltpu.VMEM((N_HEADS, LANES), F32), pltpu.VMEM((N_HEADS, ATT_WIDTH), F32),
                        pltpu.VMEM((2, pps, ATT_WIDTH, page), F32), pltpu.VMEM((2, pps, ATT_WIDTH, page), F32),
                        pltpu.SemaphoreType.DMA((2, 2))],
    )
    return pl.pallas_call(
        functools.partial(_attn_sample_body, pps=pps, n_pages=n_pages),
        grid_spec=grid_spec,
        out_shape=jax.ShapeDtypeStruct((db, 1, ATT_WIDTH), F32),
        compiler_params=_params(("arbitrary", "arbitrary")),
        name="attn_sample",
    )(page_table.reshape(-1), q_bd, bias, cache_k, cache_v)


def _s5_prep_body(lr_ref, li_ref, ldt_ref, br_ref, bi_ref, ar_ref, ai_ref, bbr_ref, bbi_ref):
    lr, li = lr_ref[...], li_ref[...]
    dt = jnp.exp(ldt_ref[...])
    mag = jnp.exp(lr * dt)
    ang = li * dt
    ab_re, ab_im = mag * jnp.cos(ang), mag * jnp.sin(ang)
    den = lr * lr + li * li
    num_re = ab_re - 1.0
    coef_re = (num_re * lr + ab_im * li) / den
    coef_im = (ab_im * lr - num_re * li) / den
    br, bi = br_ref[...], bi_ref[...]
    ar_ref[...] = ab_re
    ai_ref[...] = ab_im
    bbr_ref[...] = coef_re * br - coef_im * bi
    bbi_ref[...] = coef_re * bi + coef_im * br


def _s5_prep(lam_re, lam_im, log_dt, b_re, b_im):
    g, n = lam_re.shape
    v3 = lambda a: a.reshape(g, 1, n)
    bt = lambda a: jnp.swapaxes(a, 1, 2)
    small = jax.ShapeDtypeStruct((g, 1, n), F32)
    big = jax.ShapeDtypeStruct((g, SSM_GROUP, n), F32)
    return pl.pallas_call(
        _s5_prep_body, out_shape=[small, small, big, big], name="s5_prep",
    )(v3(lam_re), v3(lam_im), jnp.broadcast_to(log_dt[:, None, None], (g, 1, n)), bt(b_re), bt(b_im))


def _s5_body(u_ref, um_ref, h0_ref, bbd_ref, cre_ref, cim_ref, are_ref, aim_ref, d_ref,
             wglu_ref, bglu_ref, g_ref, s_ref, hf_ref, st_ref, bu_ref, *, tc, nb, n_meta, lane_w):
    i = pl.program_id(0)
    rows = tc * nb
    precision = lax.Precision.HIGHEST if bbd_ref.dtype == F32 else None

    def drive(uh, h):
        return _dot(uh.astype(bbd_ref.dtype), bbd_ref[h], precision=precision)

    @pl.when(i == 0)
    def _():
        st_ref[...] = h0_ref[...]
        for h in range(2 if n_meta else 0):
            bum = drive(um_ref[:, h * HALF_IN:(h + 1) * HALF_IN], h)
            ar, ai = are_ref[h], aim_ref[h]
            xr, xi = st_ref[h, 0], st_ref[h, 1]
            for t in range(n_meta):
                br, bi = bum[t:t + 1, :HALF_STATE], bum[t:t + 1, HALF_STATE:]
                xr, xi = ar * xr - ai * xi + br, ar * xi + ai * xr + bi
            st_ref[h, 0] = xr
            st_ref[h, 1] = xi

    u = u_ref[...].reshape(rows, SSM_WIDTH)
    ys = []
    for h in range(2):
        bu = drive(u[:, h * HALF_IN:(h + 1) * HALF_IN], h)
        bu_ref[0] = bu[:, :HALF_STATE]
        bu_ref[1] = bu[:, HALF_STATE:]
        for lc in range(HALF_STATE // lane_w):
            sl = slice(lc * lane_w, (lc + 1) * lane_w)
            ar = jnp.broadcast_to(are_ref[h, :, sl], (nb, lane_w))
            ai = jnp.broadcast_to(aim_ref[h, :, sl], (nb, lane_w))

            def step(t, x, sl=sl, ar=ar, ai=ai):
                xr, xi = x
                r0 = pl.multiple_of(t * nb, nb)
                nxr = ar * xr - ai * xi + bu_ref[0, pl.ds(r0, nb), sl]
                nxi = ar * xi + ai * xr + bu_ref[1, pl.ds(r0, nb), sl]
                bu_ref[0, pl.ds(r0, nb), sl] = nxr
                bu_ref[1, pl.ds(r0, nb), sl] = nxi
                return nxr, nxi

            xr, xi = lax.fori_loop(0, tc, step, (st_ref[h, 0, :, sl], st_ref[h, 1, :, sl]))
            st_ref[h, 0, :, sl] = xr
            st_ref[h, 1, :, sl] = xi
        ys.append(_dot(bu_ref[0].astype(BF16), cre_ref[h]) - _dot(bu_ref[1].astype(BF16), cim_ref[h]))
    y = jnp.concatenate(ys, axis=1) + d_ref[...] * u
    y = jax.nn.gelu(y)
    y = y * jax.nn.sigmoid(_dot(y.astype(BF16), wglu_ref[...]) + bglu_ref[...])
    s_ref[...] = _rms(y, g_ref[...]).reshape(tc, nb, SSM_WIDTH)

    @pl.when(i == pl.num_programs(0) - 1)
    def _():
        hf_ref[...] = st_ref[...]


def _s5(u, u_meta, h0, bbd, cre, cim, a_re, a_im, d, w_glu, b_glu, g):
    t, nb, _ = u.shape
    tc = min(S5_T, t)
    n_meta = 0 if u_meta is None else u_meta.shape[0]
    if u_meta is None:
        u_meta = jnp.zeros((SUBLANES, SSM_WIDTH), F32)
    lane_w = max(LANES, min(4 * LANES, (64 * LANES) // nb))
    full = lambda a: pl.BlockSpec(a.shape, lambda i: (0,) * a.ndim)
    blk = pl.BlockSpec((tc, nb, SSM_WIDTH), lambda i: (i, 0, 0))
    args = (u, u_meta, h0, bbd, cre, cim, a_re, a_im, d, w_glu, b_glu, g)
    return pl.pallas_call(
        functools.partial(_s5_body, tc=tc, nb=nb, n_meta=n_meta, lane_w=lane_w),
        grid=(t // tc,),
        in_specs=[blk] + [full(a) for a in args[1:]],
        out_specs=[blk, full(h0)],
        out_shape=[jax.ShapeDtypeStruct(u.shape, F32), jax.ShapeDtypeStruct(h0.shape, F32)],
        scratch_shapes=[pltpu.VMEM(h0.shape, F32), pltpu.VMEM((2, tc * nb, HALF_STATE), F32)],
        compiler_params=_params(("arbitrary",)),
        name="s5",
    )(*args)


def _block_diag_in(bb):
    eye = jnp.eye(HALF_GROUPS, dtype=F32)
    bb = bb.reshape(2, HALF_GROUPS, SSM_GROUP, STATE_DIM)
    return jnp.einsum('hgcn,gk->hgckn', bb, eye).reshape(2, HALF_IN, HALF_STATE)


def _block_diag_out(c):
    eye = jnp.eye(HALF_GROUPS, dtype=F32)
    c = c.reshape(2, HALF_GROUPS, SSM_GROUP, STATE_DIM)
    return jnp.einsum('hgcn,kg->hkngc', c, eye).reshape(2, HALF_STATE, HALF_IN)


def _state_in(s):
    nb = s.shape[0]
    return jnp.swapaxes(s.reshape(nb, 2, HALF_STATE), 0, 1)


def _state_out(s):
    nb = s.shape[1]
    return jnp.swapaxes(s, 0, 1).reshape(nb, N_GROUPS, STATE_DIM)


def _merge_body(xp_ref, ap_ref, sp_ref, xs_ref, as_ref, ss_ref, gatt_ref, wout_ref, gffn_ref,
                wr_ref, br_ref, h1_ref, f3_ref, idx_ref, wts_ref, rank_ref, cnt_ref, run_ref, logit_ref,
                *, nb, n_blk):
    i = pl.program_id(0)
    rows = nb * MERGE_T
    is_sample = i >= n_blk
    slot = i % 2

    @pl.when(i == 0)
    def _():
        run_ref[...] = jnp.zeros_like(run_ref)
        logit_ref[...] = jnp.zeros_like(logit_ref)

    s_bt = jnp.concatenate([sp_ref[:, b, :] for b in range(nb)], axis=0)
    x = jnp.where(is_sample, xs_ref[...], xp_ref[...].reshape(rows, D_MODEL))
    att = jnp.where(is_sample, as_ref[...], ap_ref[...].reshape(rows, ATT_WIDTH))
    ssm = jnp.where(is_sample, ss_ref[...], s_bt)
    cat = jnp.concatenate([_rms(att, gatt_ref[...]), ssm], axis=1).astype(BF16)
    h1 = x + _dot(cat, wout_ref[...])
    f = _rms(h1, gffn_ref[...])
    h1_ref[...] = h1
    _store_rows(f3_ref, f, rows)

    logits = logit_ref[1 - slot]
    logit_ref[slot] = _dot(f, wr_ref[...], precision=lax.Precision.HIGHEST) + br_ref[...]
    routed = (i >= 1).astype(F32)
    lane = lax.broadcasted_iota(I32, logits.shape, 1)
    vals, ids, hots = [], [], []
    for _ in range(TOP_K):
        m = jnp.max(logits, axis=1, keepdims=True)
        sel = jnp.min(jnp.where(logits == m, lane, N_EXPERTS), axis=1, keepdims=True)
        hot = lane == sel
        vals.append(m)
        ids.append(sel)
        hots.append(hot)
        logits = jnp.where(hot, -jnp.inf, logits)
    es = [jnp.exp(v - vals[0]) for v in vals]
    den = es[0] + es[1] + es[2] + es[3]

    chosen = (hots[0] | hots[1] | hots[2] | hots[3])
    tr = lax.broadcasted_iota(I32, (rows, rows), 0)
    tc = lax.broadcasted_iota(I32, (rows, rows), 1)
    before = _dot((tc < tr).astype(BF16), chosen.astype(BF16)) + run_ref[...]
    run_ref[...] = run_ref[...] + routed * jnp.sum(chosen.astype(F32), axis=0, keepdims=True)
    cnt_ref[...] = run_ref[...]

    out_lane = lax.broadcasted_iota(I32, (rows, LANES), 1)
    idx_o = jnp.zeros((rows, LANES), I32)
    wts_o = jnp.zeros((rows, LANES), F32)
    rank_o = jnp.zeros((rows, LANES), I32)
    for k in range(TOP_K):
        rank_k = jnp.sum(jnp.where(hots[k], before, 0.0), axis=1, keepdims=True).astype(I32)
        idx_o = jnp.where(out_lane == k, ids[k], idx_o)
        wts_o = jnp.where(out_lane == k, es[k] / den, wts_o)
        rank_o = jnp.where(out_lane == k, rank_k, rank_o)
    idx_ref[...] = idx_o
    wts_ref[...] = wts_o
    rank_ref[...] = rank_o


def _merge(x_p, att_p, s_p, x_s, att_s, s_s, g_att, w_out, g_ffn, w_r, b_r):
    nb, t, _ = x_p.shape
    rows = nb * MERGE_T
    n_blk = t // MERGE_T
    n_tok = (n_blk + 1) * rows
    pj = lambda i: jnp.minimum(i, n_blk - 1)
    full = lambda a: pl.BlockSpec(a.shape, lambda i: (0,) * a.ndim)
    merged = lambda i: jnp.minimum(i, n_blk)
    routed = lambda i: jnp.maximum(i - 1, 0)
    row_spec = lambda w: pl.BlockSpec((rows, w), lambda i: (routed(i), 0))
    return pl.pallas_call(
        functools.partial(_merge_body, nb=nb, n_blk=n_blk),
        grid=(n_blk + 2,),
        in_specs=[pl.BlockSpec((nb, MERGE_T, D_MODEL), lambda i: (0, pj(i), 0)),
                  pl.BlockSpec((nb, MERGE_T, ATT_WIDTH), lambda i: (0, pj(i), 0)),
                  pl.BlockSpec((MERGE_T, nb, SSM_WIDTH), lambda i: (pj(i), 0, 0)),
                  full(x_s), full(att_s), full(s_s), full(g_att), full(w_out), full(g_ffn),
                  full(w_r), full(b_r)],
        out_specs=[pl.BlockSpec((rows, D_MODEL), lambda i: (merged(i), 0)),
                   pl.BlockSpec((rows * ROW_TILES, LANES), lambda i: (merged(i), 0)),
                   row_spec(LANES), row_spec(LANES), row_spec(LANES),
                   pl.BlockSpec((1, N_EXPERTS), lambda i: (0, 0))],
        out_shape=[jax.ShapeDtypeStruct((n_tok, D_MODEL), F32),
                   jax.ShapeDtypeStruct((n_tok * ROW_TILES, LANES), F32),
                   jax.ShapeDtypeStruct((n_tok, LANES), I32),
                   jax.ShapeDtypeStruct((n_tok, LANES), F32),
                   jax.ShapeDtypeStruct((n_tok, LANES), I32),
                   jax.ShapeDtypeStruct((1, N_EXPERTS), F32)],
        scratch_shapes=[pltpu.VMEM((1, N_EXPERTS), F32), pltpu.VMEM((2, rows, N_EXPERTS), F32)],
        compiler_params=_params(("arbitrary",)),
        name="merge_route",
    )(x_p, att_p, s_p, x_s, att_s, s_s, g_att, w_out, g_ffn, w_r, b_r)


def _dispatch_body(tail_ref, has_ref, nval_ref, pos_ref, f3_ref, o_hbm, zero_ref, sem,
                   *, rows, tile, n_tiles):
    i = pl.program_id(0)

    def tile_copy(start):
        return pltpu.make_async_copy(zero_ref, _tile_rows(o_hbm, start, tile), sem)

    @pl.when(i == 0)
    def _():
        zero_ref[...] = jnp.zeros_like(zero_ref)

        def start(e, c):
            @pl.when(has_ref[e] > 0)
            def _():
                tile_copy(tail_ref[e]).start()
            return c

        def wait(e, c):
            @pl.when(has_ref[e] > 0)
            def _():
                tile_copy(tail_ref[e]).wait()
            return c

        def start_unused(j, c):
            tile_copy(pl.multiple_of(j * tile, tile)).start()
            return c

        def wait_unused(j, c):
            tile_copy(pl.multiple_of(j * tile, tile)).wait()
            return c

        lax.fori_loop(0, N_EXPERTS, start, 0)
        lax.fori_loop(nval_ref[0], n_tiles, start_unused, 0)
        lax.fori_loop(0, N_EXPERTS, wait, 0)
        lax.fori_loop(nval_ref[0], n_tiles, wait_unused, 0)

    def issue(r, c):
        for k in range(TOP_K):
            pltpu.make_async_copy(_tile_rows(f3_ref, r), _tile_rows(o_hbm, pos_ref[r * TOP_K + k]),
                                  sem).start(priority=k % 2)
        return c

    lax.fori_loop(0, rows, issue, 0, unroll=ISSUE_UNROLL)
    for _ in range(TOP_K):
        pltpu.make_async_copy(f3_ref, _tile_rows(o_hbm, 0, rows), sem).wait()


def _dispatch(tail_start, has_rows, n_valid, pos, f3, rows, m_tot, tile):
    n_tok = f3.shape[0] // ROW_TILES
    grid_spec = pltpu.PrefetchScalarGridSpec(
        num_scalar_prefetch=3,
        grid=(n_tok // rows,),
        in_specs=[pl.BlockSpec((rows * TOP_K,), lambda i, *_: (i,), memory_space=pltpu.SMEM),
                  pl.BlockSpec((rows * ROW_TILES, LANES), lambda i, *_: (i, 0))],
        out_specs=pl.BlockSpec(memory_space=pl.ANY),
        scratch_shapes=[pltpu.VMEM((tile * ROW_TILES, LANES), F32), pltpu.SemaphoreType.DMA(())],
    )
    return pl.pallas_call(
        functools.partial(_dispatch_body, rows=rows, tile=tile, n_tiles=m_tot // tile),
        grid_spec=grid_spec,
        out_shape=jax.ShapeDtypeStruct((m_tot * ROW_TILES, LANES), F32),
        compiler_params=_params(("arbitrary",)),
        name="dispatch",
    )(tail_start, has_rows, n_valid, pos, f3)


def _experts_body(tidx_ref, te_ref, nval_ref, x_ref, wg_ref, wu_ref, wd_ref, bg_ref, bu_ref, bd_ref,
                  y_ref, wg_s, wu_s, wd_s):
    i = pl.program_id(0)
    e = te_ref[i]
    prev = te_ref[jnp.maximum(i - 1, 0)]
    tile = x_ref.shape[0] // ROW_TILES

    @pl.when((i == 0) | (e != prev))
    def _():
        wg_s[...] = wg_ref[0].astype(BF16)
        wu_s[...] = wu_ref[0].astype(BF16)
        wd_s[...] = wd_ref[0].astype(BF16)

    @pl.when(i < nval_ref[0])
    def _():
        x = _load_rows(x_ref, tile).astype(BF16)
        g = jnp.minimum(_dot(x, wg_s[...]) + bg_ref[0], SWIGLU_LIMIT)
        up = jnp.clip(_dot(x, wu_s[...]) + bu_ref[0], -SWIGLU_LIMIT, SWIGLU_LIMIT)
        act = g * jax.nn.sigmoid(SWIGLU_ALPHA * g) * (up + 1.0)
        y = _dot(act.astype(BF16), wd_s[...]) + bd_ref[0]
        _store_rows(y_ref, y, tile)

    @pl.when(i >= nval_ref[0])
    def _():
        y_ref[...] = jnp.zeros_like(y_ref)


def _experts(tile_idx, tile_e, n_valid, xs, w_gate, w_up, w_down, b_gate, b_up, b_down, tile):
    n_tiles = tile_idx.shape[0]
    d_ff = w_gate.shape[-1]
    row = pl.BlockSpec((tile * ROW_TILES, LANES), lambda i, ti, te, nv: (ti[i], 0))
    w_spec = lambda a: pl.BlockSpec((1,) + a.shape[1:], lambda i, ti, te, nv: (te[i], 0, 0))
    b3 = lambda b: b.reshape(N_EXPERTS, 1, b.shape[-1])
    grid_spec = pltpu.PrefetchScalarGridSpec(
        num_scalar_prefetch=3,
        grid=(n_tiles,),
        in_specs=[row, w_spec(w_gate), w_spec(w_up), w_spec(w_down),
                  w_spec(b3(b_gate)), w_spec(b3(b_up)), w_spec(b3(b_down))],
        out_specs=pl.BlockSpec((tile * ROW_TILES, LANES), lambda i, ti, te, nv: (i, 0)),
        scratch_shapes=[pltpu.VMEM((D_MODEL, d_ff), BF16), pltpu.VMEM((D_MODEL, d_ff), BF16),
                        pltpu.VMEM((d_ff, D_MODEL), BF16)],
    )
    return pl.pallas_call(
        _experts_body,
        grid_spec=grid_spec,
        out_shape=jax.ShapeDtypeStruct(xs.shape, F32),
        compiler_params=_params(("arbitrary",)),
        name="experts",
    )(tile_idx, tile_e, n_valid, xs, w_gate, w_up, w_down, b3(b_gate), b3(b_up), b3(b_down))


def _combine_body(pos_ref, next_pos_ref, wts_ref, h1_ref, g_ref, y_hbm, yp_ref, ys_ref, *scratch,
                  nb, n_blk, rows):
    bufs, sem = (scratch[:TOP_K], scratch[TOP_K:2 * TOP_K]), scratch[2 * TOP_K]
    i = pl.program_id(0)

    def gather(p_ref, slot):
        def issue(r, c):
            for k in range(TOP_K):
                pltpu.make_async_copy(_tile_rows(y_hbm, p_ref[r * TOP_K + k]),
                                      _tile_rows(bufs[slot][k], r), sem.at[slot]).start(priority=k % 2)
            return c
        lax.fori_loop(0, rows, issue, 0, unroll=ISSUE_UNROLL)

    def finish(slot):
        for k in range(TOP_K):
            pltpu.make_async_copy(_tile_rows(y_hbm, 0, rows), bufs[slot][k], sem.at[slot]).wait()
        w = wts_ref[...]
        moe = sum(w[:, k:k + 1] * _load_rows(bufs[slot][k], rows) for k in range(TOP_K))
        y = _rms(h1_ref[...] + moe, g_ref[...])

        @pl.when(i < n_blk)
        def _():
            yp_ref[...] = y.reshape(nb, MERGE_T, D_MODEL)

        @pl.when(i == n_blk)
        def _():
            ys_ref[...] = y

    @pl.when(i == 0)
    def _():
        gather(pos_ref, 0)

    for slot in range(2):
        @pl.when(i % 2 == slot)
        def _(slot=slot):
            @pl.when(i < n_blk)
            def _():
                gather(next_pos_ref, 1 - slot)
            finish(slot)


def _combine(pos, wts, h1, g, ys, nb, t):
    rows = nb * MERGE_T
    n_blk = t // MERGE_T
    return pl.pallas_call(
        functools.partial(_combine_body, nb=nb, n_blk=n_blk, rows=rows),
        grid=(n_blk + 1,),
        in_specs=[pl.BlockSpec((rows * TOP_K,), lambda i: (i,), memory_space=pltpu.SMEM),
                  pl.BlockSpec((rows * TOP_K,), lambda i: (jnp.minimum(i + 1, n_blk),),
                               memory_space=pltpu.SMEM),
                  pl.BlockSpec((rows, LANES), lambda i: (i, 0)),
                  pl.BlockSpec((rows, D_MODEL), lambda i: (i, 0)),
                  pl.BlockSpec((1, D_MODEL), lambda i: (0, 0)),
                  pl.BlockSpec(memory_space=pl.ANY)],
        out_specs=[pl.BlockSpec((nb, MERGE_T, D_MODEL), lambda i: (0, jnp.minimum(i, n_blk - 1), 0)),
                   pl.BlockSpec((rows, D_MODEL), lambda i: (0, 0))],
        out_shape=[jax.ShapeDtypeStruct((nb, t, D_MODEL), F32),
                   jax.ShapeDtypeStruct((rows, D_MODEL), F32)],
        scratch_shapes=[pltpu.VMEM((rows * ROW_TILES, LANES), F32)] * (2 * TOP_K)
                       + [pltpu.SemaphoreType.DMA((2,))],
        compiler_params=_params(("arbitrary",)),
        name="combine",
    )(pos, pos, wts, h1, g, ys)


def _moe_plan(counts, idx, rank, tile):
    n_tok = idx.shape[0]
    n_tiles = (n_tok * TOP_K) // tile + N_EXPERTS
    padded = ((counts + tile - 1) // tile) * tile
    ends = jnp.cumsum(padded)
    off = ends - padded
    pos = (off[idx] + rank).reshape(-1)
    tile_ends = ends // tile
    n_valid = tile_ends[-1]
    tile_idx = jnp.minimum(jnp.arange(n_tiles, dtype=I32), n_valid - 1)
    tile_e = jnp.minimum(jnp.sum(tile_idx[:, None] >= tile_ends[None, :], axis=1), N_EXPERTS - 1).astype(I32)
    return pos.astype(I32), tile_idx, tile_e, n_valid.reshape(1).astype(I32), (ends - tile).astype(I32), \
        (counts > 0).astype(I32), n_tiles * tile


def kernel(x_prompt, x_sample, cache_k, cache_v, page_table, state_ssm_re, state_ssm_im, meta_tokens, g_mix, w_in, sb_bias, lambda_re, lambda_im, log_dt, b_re, b_im, c_re, c_im, d_skip, w_glu, b_glu, g_att_out, g_ssm_out, w_out, g_ffn, w_router, b_router, w_gate, b_gate, w_up, b_up, w_down, b_down, g_final):
    depth = g_mix.shape[0]
    assert depth == 1, "single-layer step"
    b, t, _ = x_prompt.shape
    db, ds, _ = x_sample.shape
    n_meta = meta_tokens.shape[0]
    assert ds == 1 and db == b * MERGE_T and t % BLOCK == 0 and n_meta <= BLOCK
    l = 0
    row = lambda a: a.reshape(1, -1)

    w_in_b = w_in[l].astype(BF16)
    g_mix_l = row(g_mix[l])
    q_p, k_p, v_p, kb_p, vb_p, u_p = _project(x_prompt, g_mix_l, w_in_b, PROJ_T)
    _, k_m, v_m, kb_m, vb_m, u_m = _project(meta_tokens[None], g_mix_l, w_in_b, n_meta)
    q_s, k_s, v_s, _, _, u_s = _project(x_sample.reshape(1, db, D_MODEL), g_mix_l, w_in_b, db)

    pad_meta = lambda a: jnp.pad(a[0], ((BLOCK - n_meta, 0), (0, 0)))
    att_p = _attn_prompt(sb_bias[l], q_p, kb_p, vb_p, pad_meta(kb_m), pad_meta(vb_m), n_meta)
    n_phys, page = cache_k.shape[1], cache_k.shape[2]
    head_of_lane = jnp.arange(ATT_WIDTH) // HEAD_DIM
    q_bd = jnp.where(head_of_lane[None, None, :] == jnp.arange(N_HEADS)[None, :, None],
                     q_s[0][:, None, :], jnp.zeros((), BF16))
    keys_minor = lambda c: jnp.transpose(c, (0, 2, 3, 1)).reshape(n_phys, ATT_WIDTH, page)
    att_s = _attn_sample(page_table, q_bd, sb_bias[l].reshape(N_HEADS, 1),
                         keys_minor(cache_k[l]), keys_minor(cache_v[l]))

    ab_re, ab_im, bb_re, bb_im = _s5_prep(lambda_re[l], lambda_im[l], log_dt[l], b_re[l], b_im[l])
    bbd = jnp.concatenate([_block_diag_in(bb_re), _block_diag_in(bb_im)], axis=2)
    cre = _block_diag_out(c_re[l]).astype(BF16)
    cim = _block_diag_out(c_im[l]).astype(BF16)
    a_re = ab_re.reshape(2, 1, HALF_STATE)
    a_im = ab_im.reshape(2, 1, HALF_STATE)
    s5_w = (cre, cim, a_re, a_im, row(d_skip[l]), w_glu[l].astype(BF16), row(b_glu[l]), row(g_ssm_out[l]))
    zero_state = jnp.zeros((2, 2, b, HALF_STATE), F32)
    s_p, hf_p = _s5(u_p, u_m.reshape(n_meta, SSM_WIDTH), zero_state, bbd.astype(BF16), *s5_w)
    h0_s = jnp.stack([_state_in(state_ssm_re[l]), _state_in(state_ssm_im[l])], axis=1)
    s_s, hf_s = _s5(u_s.reshape(1, db, SSM_WIDTH), None, h0_s, bbd, *s5_w)

    h1, f3, idx, wts, rank, cnt = _merge(
        x_prompt, att_p, s_p, x_sample.reshape(db, D_MODEL), att_s.reshape(db, ATT_WIDTH),
        s_s.reshape(db, SSM_WIDTH), row(g_att_out[l]), w_out[l].astype(BF16), row(g_ffn[l]),
        w_router[l], row(b_router[l]))
    counts = cnt[0].astype(I32)
    pos, tile_idx, tile_e, n_valid, tail_start, has_rows, m_tot = _moe_plan(
        counts, idx[:, :TOP_K], rank[:, :TOP_K], EXPERT_TILE)
    rows = b * MERGE_T
    xs = _dispatch(tail_start, has_rows, n_valid, pos, f3, rows, m_tot, EXPERT_TILE)
    ys = _experts(tile_idx, tile_e, n_valid, xs, w_gate[l], w_up[l], w_down[l],
                  b_gate[l], b_up[l], b_down[l], EXPERT_TILE)
    y_prompt, y_sample = _combine(pos, wts, h1, row(g_final), ys, b, t)

    heads = lambda a: a.reshape(a.shape[:-1] + (N_HEADS, HEAD_DIM))
    with_meta = lambda m, p: heads(jnp.concatenate([jnp.broadcast_to(m, (b,) + m.shape[1:]), p], axis=1))[None]
    return (y_prompt, y_sample.reshape(db, ds, D_MODEL),
            with_meta(k_m, k_p), with_meta(v_m, v_p),
            heads(k_s.reshape(db, ds, ATT_WIDTH))[None], heads(v_s.reshape(db, ds, ATT_WIDTH))[None],
            _state_out(hf_p[:, 0])[None], _state_out(hf_p[:, 1])[None],
            _state_out(hf_s[:, 0])[None], _state_out(hf_s[:, 1])[None])
```

```python
import functools
import math

import jax
import jax.numpy as jnp
from jax import lax
from jax.experimental import pallas as pl
from jax.experimental.pallas import tpu as pltpu

F32 = jnp.float32
BF16 = jnp.bfloat16
I32 = jnp.int32

D_MODEL = 1024
ATT_WIDTH = 512
N_HEADS = 8
HEAD_DIM = 64
SSM_WIDTH = 512
SSM_GROUP = 16
N_GROUPS = 32
STATE_DIM = 64
N_EXPERTS = 32
TOP_K = 4
SWIGLU_LIMIT = 7.0
SWIGLU_ALPHA = 1.702
EPS = 1e-5
ATT_SCALE = 1.0 / math.sqrt(HEAD_DIM)
BLOCK = 128

LANES = 128
SUBLANES = 8
ROW_TILES = D_MODEL // LANES
HALF_GROUPS = N_GROUPS // 2
HALF_IN = HALF_GROUPS * SSM_GROUP
HALF_STATE = HALF_GROUPS * STATE_DIM
VMEM_LIMIT = 56 * 1024 * 1024

PROJ_T = 32
S5_T = 32
MERGE_T = SUBLANES
PAGES_PER_STEP = 16
EXPERT_TILE = 512
ISSUE_UNROLL = 4


def _rms(x, g):
    return x * lax.rsqrt(jnp.mean(x * x, axis=-1, keepdims=True) + EPS) * g


def _dot(a, b, **kw):
    return jnp.dot(a, b, preferred_element_type=F32, **kw)


def _dot_nt(a, b):
    return lax.dot_general(a, b, (((1,), (1,)), ((), ())), preferred_element_type=F32)


def _params(sem=None):
    return pltpu.CompilerParams(dimension_semantics=sem, vmem_limit_bytes=VMEM_LIMIT)


def _tile_rows(ref, r, n=1):
    return ref.at[pl.ds(pl.multiple_of(r * ROW_TILES, ROW_TILES), n * ROW_TILES)]


def _load_rows(ref, n):
    return jnp.concatenate([ref[pl.ds(c, n, stride=ROW_TILES), :] for c in range(ROW_TILES)], axis=1)


def _store_rows(ref, val, n):
    for c in range(ROW_TILES):
        ref[pl.ds(c, n, stride=ROW_TILES), :] = val[:, c * LANES:(c + 1) * LANES]


def _proj_body(x_ref, g_ref, w_ref, q_ref, k_ref, v_ref, kb_ref, vb_ref, u_ref, *, nb, tt):
    x = x_ref[...].reshape(nb * tt, D_MODEL)
    h = _rms(x, g_ref[...]).astype(BF16)
    p = _dot(h, w_ref[...])
    q = p[:, :ATT_WIDTH] * ATT_SCALE
    k = p[:, ATT_WIDTH:2 * ATT_WIDTH]
    v = p[:, 2 * ATT_WIDTH:3 * ATT_WIDTH]
    u = p[:, 3 * ATT_WIDTH:]
    q_ref[...] = q.astype(BF16).reshape(nb, tt, ATT_WIDTH)
    k_ref[...] = k.reshape(nb, tt, ATT_WIDTH)
    v_ref[...] = v.reshape(nb, tt, ATT_WIDTH)
    kb_ref[...] = k.astype(BF16).reshape(nb, tt, ATT_WIDTH)
    vb_ref[...] = v.astype(BF16).reshape(nb, tt, ATT_WIDTH)
    for b in range(nb):
        u_ref[:, b, :] = u[b * tt:(b + 1) * tt, :]


def _project(x, g, w, tt):
    nb, t, _ = x.shape
    spec_x = pl.BlockSpec((nb, tt, D_MODEL), lambda i: (0, i, 0))
    spec_a = pl.BlockSpec((nb, tt, ATT_WIDTH), lambda i: (0, i, 0))
    shp = lambda dt: jax.ShapeDtypeStruct((nb, t, ATT_WIDTH), dt)
    return pl.pallas_call(
        functools.partial(_proj_body, nb=nb, tt=tt),
        grid=(t // tt,),
        in_specs=[spec_x,
                  pl.BlockSpec((1, D_MODEL), lambda i: (0, 0)),
                  pl.BlockSpec(w.shape, lambda i: (0, 0))],
        out_specs=[spec_a, spec_a, spec_a, spec_a, spec_a,
                   pl.BlockSpec((tt, nb, SSM_WIDTH), lambda i: (i, 0, 0))],
        out_shape=[shp(BF16), shp(F32), shp(F32), shp(BF16), shp(BF16),
                   jax.ShapeDtypeStruct((t, nb, SSM_WIDTH), F32)],
        compiler_params=_params(("arbitrary",)),
        name="proj",
    )(x, g, w)


LOG2E = 1.0 / math.log(2.0)


def _softplus(z):
    return jnp.maximum(z, 0.0) + jnp.log2(1.0 + jnp.exp2(-jnp.abs(z)))


def _sb_split(z, valid):
    sp = _softplus(z)
    if valid is not None:
        sp = jnp.where(valid, sp, 0.0)
    hi = sp.astype(BF16)
    lo = (sp - hi.astype(F32)).astype(BF16)
    return jnp.concatenate([hi, lo], axis=1)


def _sb_weights(z, ext, valid, carry):
    a = jnp.exp2(z - ext[:, :BLOCK] - carry)
    if valid is not None:
        a = jnp.where(valid, a, 0.0)
    return a.astype(BF16), carry + ext[:, BLOCK:]


def _suffix_matrix():
    j = lax.broadcasted_iota(I32, (2 * BLOCK, 2 * BLOCK), 0)
    s = lax.broadcasted_iota(I32, (2 * BLOCK, 2 * BLOCK), 1)
    j = jnp.where(j >= BLOCK, j - BLOCK, j)
    return ((s >= BLOCK) | (j >= s)).astype(BF16)


def _attn_prompt_body(bias_ref, q_ref, k_ref, v_ref, km_ref, vm_ref, bm_ref, o_ref, qm_ref, carry_ref, acc_ref,
                      *, n_meta):
    jq = pl.program_id(1)
    m2 = _suffix_matrix()
    row = lax.broadcasted_iota(I32, (BLOCK, BLOCK), 0)
    col = lax.broadcasted_iota(I32, (BLOCK, LANES), 1)
    causal = col < row
    low_half = col < HEAD_DIM
    pair_lanes = lambda p: slice(p * LANES, (p + 1) * LANES)

    for h in range(N_HEADS):
        q_pair = q_ref[0, :, pair_lanes(h // 2)]
        qm_ref[h] = jnp.where(low_half == (h % 2 == 0), q_pair, jnp.zeros_like(q_pair))
    carry_ref[...] = jnp.zeros_like(carry_ref)
    acc_ref[...] = jnp.zeros_like(acc_ref)

    def blocks(sources, valid):
        heads = range(N_HEADS)
        zs = [[(_dot_nt(qm_ref[h], k_of(pair_lanes(h // 2))) + bias_ref[h]) * LOG2E for h in heads]
              for k_of, _ in sources]
        exts = [[_dot(_sb_split(z, valid), m2) for z in zb] for zb in zs]
        pv = []
        for h in heads:
            carry, out = carry_ref[h], None
            for s, (_, v_of) in enumerate(sources):
                a, carry = _sb_weights(zs[s][h], exts[s][h], valid, carry)
                d = _dot(a, v_of(pair_lanes(h // 2)))
                out = d if out is None else out + d
            carry_ref[h] = carry
            pv.append(out)
        for p in range(N_HEADS // 2):
            acc_ref[:, pair_lanes(p)] += jnp.where(low_half, pv[2 * p], pv[2 * p + 1])

    def key_block(j):
        rows = pl.ds(pl.multiple_of(j * BLOCK, BLOCK), BLOCK)
        return (lambda ls: k_ref[0, rows, ls]), (lambda ls: v_ref[0, rows, ls])

    blocks([key_block(jq)], causal)

    def older_pair(i, c):
        blocks([key_block(jq - 1 - 2 * i), key_block(jq - 2 - 2 * i)], None)
        return c

    lax.fori_loop(0, jq // 2, older_pair, 0)

    @pl.when(jq % 2 == 1)
    def _():
        blocks([key_block(0)], None)

    zm = (_dot(q_ref[0], km_ref[...]) + bm_ref[...]) * LOG2E
    mj = lax.broadcasted_iota(I32, (2 * LANES, LANES), 0) % LANES
    ms = lax.broadcasted_iota(I32, (2 * LANES, LANES), 1)
    same_head_suffix = ((mj // n_meta == ms // n_meta) & (mj >= ms)).astype(BF16)
    suffix = _dot(_sb_split(zm, None), same_head_suffix)
    carry = jnp.zeros((BLOCK, LANES), F32)
    for h in range(N_HEADS):
        carry = jnp.where(col // n_meta == h, carry_ref[h], carry)
    am = jnp.exp2(zm - suffix - carry).astype(BF16)
    o_ref[0] = acc_ref[...] + _dot(am, vm_ref[...])


def _attn_prompt(bias, q, k, v, km, vm, n_meta):
    b, t, _ = q.shape
    assert n_meta * N_HEADS == LANES
    eye = jnp.eye(N_HEADS, dtype=km.dtype)
    per_head = lambda a: a.reshape(n_meta, N_HEADS, HEAD_DIM)
    km = jnp.einsum('shd,hg->hdgs', per_head(km), eye).reshape(ATT_WIDTH, LANES)
    vm = jnp.einsum('shd,hg->gshd', per_head(vm), eye).reshape(LANES, ATT_WIDTH)
    bm = jnp.repeat(bias, n_meta).reshape(1, LANES)
    return pl.pallas_call(
        functools.partial(_attn_prompt_body, n_meta=n_meta),
        grid=(b, t // BLOCK),
        in_specs=[pl.BlockSpec(memory_space=pltpu.SMEM),
                  pl.BlockSpec((1, BLOCK, ATT_WIDTH), lambda i, j: (i, j, 0)),
                  pl.BlockSpec((1, t, ATT_WIDTH), lambda i, j: (i, 0, 0)),
                  pl.BlockSpec((1, t, ATT_WIDTH), lambda i, j: (i, 0, 0)),
                  pl.BlockSpec((ATT_WIDTH, LANES), lambda i, j: (0, 0)),
                  pl.BlockSpec((LANES, ATT_WIDTH), lambda i, j: (0, 0)),
                  pl.BlockSpec((1, LANES), lambda i, j: (0, 0))],
        out_specs=pl.BlockSpec((1, BLOCK, ATT_WIDTH), lambda i, j: (i, j, 0)),
        out_shape=jax.ShapeDtypeStruct((b, t, ATT_WIDTH), F32),
        scratch_shapes=[pltpu.VMEM((N_HEADS, BLOCK, LANES), BF16),
                        pltpu.VMEM((N_HEADS, BLOCK, LANES), F32),
                        pltpu.VMEM((BLOCK, ATT_WIDTH), F32)],
        compiler_params=_params(("arbitrary", "arbitrary")),
        name="attn_prompt",
    )(bias, q, k, v, km, vm, bm)


def _attn_sample_body(pt_ref, q_ref, bias_ref, k_hbm, v_hbm, o_ref, carry_ref, acc_ref, kbuf, vbuf, sem,
                      *, pps, n_pages):
    b, c = pl.program_id(0), pl.program_id(1)
    n_chunks = pl.num_programs(1)
    step = b * n_chunks + c
    slot = step % 2

    def fetch(step, slot):
        sb, sc = step // n_chunks, step % n_chunks
        base = sb * n_pages + (n_chunks - 1 - sc) * pps
        for i in range(pps):
            page = pt_ref[base + i]
            pltpu.make_async_copy(k_hbm.at[page], kbuf.at[slot, i], sem.at[0, slot]).start()
            pltpu.make_async_copy(v_hbm.at[page], vbuf.at[slot, i], sem.at[1, slot]).start()

    @pl.when(step == 0)
    def _():
        fetch(step, slot)

    @pl.when(step + 1 < pl.num_programs(0) * n_chunks)
    def _():
        fetch(step + 1, 1 - slot)

    @pl.when(c == 0)
    def _():
        carry_ref[...] = jnp.zeros_like(carry_ref)
        acc_ref[...] = jnp.zeros_like(acc_ref)

    pltpu.make_async_copy(k_hbm.at[pl.ds(0, pps)], kbuf.at[slot], sem.at[0, slot]).wait()
    pltpu.make_async_copy(v_hbm.at[pl.ds(0, pps)], vbuf.at[slot], sem.at[1, slot]).wait()

    q = q_ref[0]
    order = list(reversed(range(pps)))
    z = jnp.concatenate([(_dot(q, kbuf[slot, i].astype(BF16)) + bias_ref[...]) * LOG2E for i in order], axis=0)
    ext = _dot(_sb_split(z, None), _suffix_matrix())
    carry = carry_ref[...]
    carries = []
    for j in range(pps):
        carries.append(carry)
        carry = carry + ext[j * N_HEADS:(j + 1) * N_HEADS, BLOCK:]
    a = jnp.exp2(z - ext[:, :BLOCK] - jnp.concatenate(carries, axis=0))
    acc = acc_ref[...]
    for j, i in enumerate(order):
        a_j = a[j * N_HEADS:(j + 1) * N_HEADS].astype(BF16)
        acc = acc + _dot_nt(a_j, vbuf[slot, i].astype(BF16))
    carry_ref[...] = carry
    acc_ref[...] = acc

    @pl.when(c == n_chunks - 1)
    def _():
        head = lax.broadcasted_iota(I32, acc.shape, 0)
        lane = lax.broadcasted_iota(I32, acc.shape, 1)
        own = lane // HEAD_DIM == head
        o_ref[0] = jnp.sum(jnp.where(own, acc, 0.0), axis=0, keepdims=True)


def _attn_sample(page_table, q_bd, bias, cache_k, cache_v):
    db, n_pages = page_table.shape
    page = cache_k.shape[2]
    assert page == BLOCK
    pps = min(PAGES_PER_STEP, n_pages)
    n_chunks = n_pages // pps
    grid_spec = pltpu.PrefetchScalarGridSpec(
        num_scalar_prefetch=1,
        grid=(db, n_chunks),
        in_specs=[pl.BlockSpec((1, N_HEADS, ATT_WIDTH), lambda b, c, pt: (b, 0, 0)),
                  pl.BlockSpec((N_HEADS, 1), lambda b, c, pt: (0, 0)),
                  pl.BlockSpec(memory_space=pl.ANY), pl.BlockSpec(memory_space=pl.ANY)],
        out_specs=pl.BlockSpec((1, 1, ATT_WIDTH), lambda b, c, pt: (b, 0, 0)),
        scratch_shapes=[pltpu.VMEM((N_HEADS, LANES), F32), pltpu.VMEM((N_HEADS, ATT_WIDTH), F32),
                        pltpu.VMEM((2, pps, ATT_WIDTH, page), F32), pltpu.VMEM((2, pps, ATT_WIDTH, page), F32),
                        pltpu.SemaphoreType.DMA((2, 2))],
    )
    return pl.pallas_call(
        functools.partial(_attn_sample_body, pps=pps, n_pages=n_pages),
        grid_spec=grid_spec,
        out_shape=jax.ShapeDtypeStruct((db, 1, ATT_WIDTH), F32),
        compiler_params=_params(("arbitrary", "arbitrary")),
        name="attn_sample",
    )(page_table.reshape(-1), q_bd, bias, cache_k, cache_v)


def _s5_prep_body(lr_ref, li_ref, ldt_ref, br_ref, bi_ref, ar_ref, ai_ref, bbr_ref, bbi_ref):
    lr, li = lr_ref[...], li_ref[...]
    dt = jnp.exp(ldt_ref[...])
    mag = jnp.exp(lr * dt)
    ang = li * dt
    ab_re, ab_im = mag * jnp.cos(ang), mag * jnp.sin(ang)
    den = lr * lr + li * li
    num_re = ab_re - 1.0
    coef_re = (num_re * lr + ab_im * li) / den
    coef_im = (ab_im * lr - num_re * li) / den
    br, bi = br_ref[...], bi_ref[...]
    ar_ref[...] = ab_re
    ai_ref[...] = ab_im
    bbr_ref[...] = coef_re * br - coef_im * bi
    bbi_ref[...] = coef_re * bi + coef_im * br


def _s5_prep(lam_re, lam_im, log_dt, b_re, b_im):
    g, n = lam_re.shape
    v3 = lambda a: a.reshape(g, 1, n)
    bt = lambda a: jnp.swapaxes(a, 1, 2)
    small = jax.ShapeDtypeStruct((g, 1, n), F32)
    big = jax.ShapeDtypeStruct((g, SSM_GROUP, n), F32)
    return pl.pallas_call(
        _s5_prep_body, out_shape=[small, small, big, big], name="s5_prep",
    )(v3(lam_re), v3(lam_im), jnp.broadcast_to(log_dt[:, None, None], (g, 1, n)), bt(b_re), bt(b_im))


def _s5_body(u_ref, um_ref, h0_ref, bbd_ref, cre_ref, cim_ref, are_ref, aim_ref, d_ref,
             wglu_ref, bglu_ref, g_ref, s_ref, hf_ref, st_ref, bu_ref, *, tc, nb, n_meta, lane_w):
    i = pl.program_id(0)
    rows = tc * nb
    precision = lax.Precision.HIGHEST if bbd_ref.dtype == F32 else None

    def drive(uh, h):
        return _dot(uh.astype(bbd_ref.dtype), bbd_ref[h], precision=precision)

    @pl.when(i == 0)
    def _():
        st_ref[...] = h0_ref[...]
        for h in range(2 if n_meta else 0):
            bum = drive(um_ref[:, h * HALF_IN:(h + 1) * HALF_IN], h)
            ar, ai = are_ref[h], aim_ref[h]
            xr, xi = st_ref[h, 0], st_ref[h, 1]
            for t in range(n_meta):
                br, bi = bum[t:t + 1, :HALF_STATE], bum[t:t + 1, HALF_STATE:]
                xr, xi = ar * xr - ai * xi + br, ar * xi + ai * xr + bi
            st_ref[h, 0] = xr
            st_ref[h, 1] = xi

    u = u_ref[...].reshape(rows, SSM_WIDTH)
    ys = []
    for h in range(2):
        bu = drive(u[:, h * HALF_IN:(h + 1) * HALF_IN], h)
        bu_ref[0] = bu[:, :HALF_STATE]
        bu_ref[1] = bu[:, HALF_STATE:]
        for lc in range(HALF_STATE // lane_w):
            sl = slice(lc * lane_w, (lc + 1) * lane_w)
            ar = jnp.broadcast_to(are_ref[h, :, sl], (nb, lane_w))
            ai = jnp.broadcast_to(aim_ref[h, :, sl], (nb, lane_w))

            def step(t, x, sl=sl, ar=ar, ai=ai):
                xr, xi = x
                r0 = pl.multiple_of(t * nb, nb)
                nxr = ar * xr - ai * xi + bu_ref[0, pl.ds(r0, nb), sl]
                nxi = ar * xi + ai * xr + bu_ref[1, pl.ds(r0, nb), sl]
                bu_ref[0, pl.ds(r0, nb), sl] = nxr
                bu_ref[1, pl.ds(r0, nb), sl] = nxi
                return nxr, nxi

            xr, xi = lax.fori_loop(0, tc, step, (st_ref[h, 0, :, sl], st_ref[h, 1, :, sl]))
            st_ref[h, 0, :, sl] = xr
            st_ref[h, 1, :, sl] = xi
        ys.append(_dot(bu_ref[0].astype(BF16), cre_ref[h]) - _dot(bu_ref[1].astype(BF16), cim_ref[h]))
    y = jnp.concatenate(ys, axis=1) + d_ref[...] * u
    y = jax.nn.gelu(y)
    y = y * jax.nn.sigmoid(_dot(y.astype(BF16), wglu_ref[...]) + bglu_ref[...])
    s_ref[...] = _rms(y, g_ref[...]).reshape(tc, nb, SSM_WIDTH)

    @pl.when(i == pl.num_programs(0) - 1)
    def _():
        hf_ref[...] = st_ref[...]


def _s5(u, u_meta, h0, bbd, cre, cim, a_re, a_im, d, w_glu, b_glu, g):
    t, nb, _ = u.shape
    tc = min(S5_T, t)
    n_meta = 0 if u_meta is None else u_meta.shape[0]
    if u_meta is None:
        u_meta = jnp.zeros((SUBLANES, SSM_WIDTH), F32)
    lane_w = max(LANES, min(4 * LANES, (64 * LANES) // nb))
    full = lambda a: pl.BlockSpec(a.shape, lambda i: (0,) * a.ndim)
    blk = pl.BlockSpec((tc, nb, SSM_WIDTH), lambda i: (i, 0, 0))
    args = (u, u_meta, h0, bbd, cre, cim, a_re, a_im, d, w_glu, b_glu, g)
    return pl.pallas_call(
        functools.partial(_s5_body, tc=tc, nb=nb, n_meta=n_meta, lane_w=lane_w),
        grid=(t // tc,),
        in_specs=[blk] + [full(a) for a in args[1:]],
        out_specs=[blk, full(h0)],
        out_shape=[jax.ShapeDtypeStruct(u.shape, F32), jax.ShapeDtypeStruct(h0.shape, F32)],
        scratch_shapes=[pltpu.VMEM(h0.shape, F32), pltpu.VMEM((2, tc * nb, HALF_STATE), F32)],
        compiler_params=_params(("arbitrary",)),
        name="s5",
    )(*args)


def _block_diag_in(bb):
    eye = jnp.eye(HALF_GROUPS, dtype=F32)
    bb = bb.reshape(2, HALF_GROUPS, SSM_GROUP, STATE_DIM)
    return jnp.einsum('hgcn,gk->hgckn', bb, eye).reshape(2, HALF_IN, HALF_STATE)


def _block_diag_out(c):
    eye = jnp.eye(HALF_GROUPS, dtype=F32)
    c = c.reshape(2, HALF_GROUPS, SSM_GROUP, STATE_DIM)
    return jnp.einsum('hgcn,kg->hkngc', c, eye).reshape(2, HALF_STATE, HALF_IN)


def _state_in(s):
    nb = s.shape[0]
    return jnp.swapaxes(s.reshape(nb, 2, HALF_STATE), 0, 1)


def _state_out(s):
    nb = s.shape[1]
    return jnp.swapaxes(s, 0, 1).reshape(nb, N_GROUPS, STATE_DIM)


def _merge_body(xp_ref, ap_ref, sp_ref, xs_ref, as_ref, ss_ref, gatt_ref, wout_ref, gffn_ref,
                wr_ref, br_ref, h1_ref, f3_ref, idx_ref, wts_ref, rank_ref, cnt_ref, run_ref, logit_ref,
                *, nb, n_blk):
    i = pl.program_id(0)
    rows = nb * MERGE_T
    is_sample = i >= n_blk
    slot = i % 2

    @pl.when(i == 0)
    def _():
        run_ref[...] = jnp.zeros_like(run_ref)
        logit_ref[...] = jnp.zeros_like(logit_ref)

    s_bt = jnp.concatenate([sp_ref[:, b, :] for b in range(nb)], axis=0)
    x = jnp.where(is_sample, xs_ref[...], xp_ref[...].reshape(rows, D_MODEL))
    att = jnp.where(is_sample, as_ref[...], ap_ref[...].reshape(rows, ATT_WIDTH))
    ssm = jnp.where(is_sample, ss_ref[...], s_bt)
    cat = jnp.concatenate([_rms(att, gatt_ref[...]), ssm], axis=1).astype(BF16)
    h1 = x + _dot(cat, wout_ref[...])
    f = _rms(h1, gffn_ref[...])
    h1_ref[...] = h1
    _store_rows(f3_ref, f, rows)

    logits = logit_ref[1 - slot]
    logit_ref[slot] = _dot(f, wr_ref[...], precision=lax.Precision.HIGHEST) + br_ref[...]
    routed = (i >= 1).astype(F32)
    lane = lax.broadcasted_iota(I32, logits.shape, 1)
    vals, ids, hots = [], [], []
    for _ in range(TOP_K):
        m = jnp.max(logits, axis=1, keepdims=True)
        sel = jnp.min(jnp.where(logits == m, lane, N_EXPERTS), axis=1, keepdims=True)
        hot = lane == sel
        vals.append(m)
        ids.append(sel)
        hots.append(hot)
        logits = jnp.where(hot, -jnp.inf, logits)
    es = [jnp.exp(v - vals[0]) for v in vals]
    den = es[0] + es[1] + es[2] + es[3]

    chosen = (hots[0] | hots[1] | hots[2] | hots[3])
    tr = lax.broadcasted_iota(I32, (rows, rows), 0)
    tc = lax.broadcasted_iota(I32, (rows, rows), 1)
    before = _dot((tc < tr).astype(BF16), chosen.astype(BF16)) + run_ref[...]
    run_ref[...] = run_ref[...] + routed * jnp.sum(chosen.astype(F32), axis=0, keepdims=True)
    cnt_ref[...] = run_ref[...]

    out_lane = lax.broadcasted_iota(I32, (rows, LANES), 1)
    idx_o = jnp.zeros((rows, LANES), I32)
    wts_o = jnp.zeros((rows, LANES), F32)
    rank_o = jnp.zeros((rows, LANES), I32)
    for k in range(TOP_K):
        rank_k = jnp.sum(jnp.where(hots[k], before, 0.0), axis=1, keepdims=True).astype(I32)
        idx_o = jnp.where(out_lane == k, ids[k], idx_o)
        wts_o = jnp.where(out_lane == k, es[k] / den, wts_o)
        rank_o = jnp.where(out_lane == k, rank_k, rank_o)
    idx_ref[...] = idx_o
    wts_ref[...] = wts_o
    rank_ref[...] = rank_o


def _merge(x_p, att_p, s_p, x_s, att_s, s_s, g_att, w_out, g_ffn, w_r, b_r):
    nb, t, _ = x_p.shape
    rows = nb * MERGE_T
    n_blk = t // MERGE_T
    n_tok = (n_blk + 1) * rows
    pj = lambda i: jnp.minimum(i, n_blk - 1)
    full = lambda a: pl.BlockSpec(a.shape, lambda i: (0,) * a.ndim)
    merged = lambda i: jnp.minimum(i, n_blk)
    routed = lambda i: jnp.maximum(i - 1, 0)
    row_spec = lambda w: pl.BlockSpec((rows, w), lambda i: (routed(i), 0))
    return pl.pallas_call(
        functools.partial(_merge_body, nb=nb, n_blk=n_blk),
        grid=(n_blk + 2,),
        in_specs=[pl.BlockSpec((nb, MERGE_T, D_MODEL), lambda i: (0, pj(i), 0)),
                  pl.BlockSpec((nb, MERGE_T, ATT_WIDTH), lambda i: (0, pj(i), 0)),
                  pl.BlockSpec((MERGE_T, nb, SSM_WIDTH), lambda i: (pj(i), 0, 0)),
                  full(x_s), full(att_s), full(s_s), full(g_att), full(w_out), full(g_ffn),
                  full(w_r), full(b_r)],
        out_specs=[pl.BlockSpec((rows, D_MODEL), lambda i: (merged(i), 0)),
                   pl.BlockSpec((rows * ROW_TILES, LANES), lambda i: (merged(i), 0)),
                   row_spec(LANES), row_spec(LANES), row_spec(LANES),
                   pl.BlockSpec((1, N_EXPERTS), lambda i: (0, 0))],
        out_shape=[jax.ShapeDtypeStruct((n_tok, D_MODEL), F32),
                   jax.ShapeDtypeStruct((n_tok * ROW_TILES, LANES), F32),
                   jax.ShapeDtypeStruct((n_tok, LANES), I32),
                   jax.ShapeDtypeStruct((n_tok, LANES), F32),
                   jax.ShapeDtypeStruct((n_tok, LANES), I32),
                   jax.ShapeDtypeStruct((1, N_EXPERTS), F32)],
        scratch_shapes=[pltpu.VMEM((1, N_EXPERTS), F32), pltpu.VMEM((2, rows, N_EXPERTS), F32)],
        compiler_params=_params(("arbitrary",)),
        name="merge_route",
    )(x_p, att_p, s_p, x_s, att_s, s_s, g_att, w_out, g_ffn, w_r, b_r)


def _dispatch_body(tail_ref, has_ref, nval_ref, pos_ref, f3_hbm, o_hbm, zero_ref, buf_ref, sem, load_sem,
                   scatter_sem, *, rows, tile, n_tiles):
    i = pl.program_id(0)
    n = pl.num_programs(0)
    slot = i % 3

    def load(blk):
        s = blk % 3
        return pltpu.make_async_copy(_tile_rows(f3_hbm, blk * rows, rows), buf_ref.at[s], load_sem.at[s])

    def wait_scatter(s):
        for _ in range(TOP_K):
            pltpu.make_async_copy(buf_ref.at[s], _tile_rows(o_hbm, 0, rows), scatter_sem.at[s]).wait()

    def tile_copy(start):
        return pltpu.make_async_copy(zero_ref, _tile_rows(o_hbm, start, tile), sem)

    @pl.when(i == 0)
    def _():
        zero_ref[...] = jnp.zeros_like(zero_ref)

        def start(e, c):
            @pl.when(has_ref[e] > 0)
            def _():
                tile_copy(tail_ref[e]).start()
            return c

        def wait(e, c):
            @pl.when(has_ref[e] > 0)
            def _():
                tile_copy(tail_ref[e]).wait()
            return c

        def start_unused(j, c):
            tile_copy(pl.multiple_of(j * tile, tile)).start()
            return c

        def wait_unused(j, c):
            tile_copy(pl.multiple_of(j * tile, tile)).wait()
            return c

        lax.fori_loop(0, N_EXPERTS, start, 0)
        lax.fori_loop(nval_ref[0], n_tiles, start_unused, 0)
        lax.fori_loop(0, N_EXPERTS, wait, 0)
        lax.fori_loop(nval_ref[0], n_tiles, wait_unused, 0)
        load(i).start()

    @pl.when(i + 1 < n)
    def _():
        load(i + 1).start()

    load(i).wait()
    src = buf_ref.at[slot]

    def issue(r, c):
        for k in range(TOP_K):
            pltpu.make_async_copy(_tile_rows(src, r), _tile_rows(o_hbm, pos_ref[r * TOP_K + k]),
                                  scatter_sem.at[slot]).start(priority=k % 2)
        return c

    lax.fori_loop(0, rows, issue, 0, unroll=ISSUE_UNROLL)

    @pl.when(i >= 1)
    def _():
        wait_scatter((i + 2) % 3)

    @pl.when(i == n - 1)
    def _():
        wait_scatter(slot)


def _dispatch(tail_start, has_rows, n_valid, pos, f3, rows, m_tot, tile):
    n_tok = f3.shape[0] // ROW_TILES
    grid_spec = pltpu.PrefetchScalarGridSpec(
        num_scalar_prefetch=3,
        grid=(n_tok // rows,),
        in_specs=[pl.BlockSpec((rows * TOP_K,), lambda i, *_: (i,), memory_space=pltpu.SMEM),
                  pl.BlockSpec(memory_space=pl.ANY)],
        out_specs=pl.BlockSpec(memory_space=pl.ANY),
        scratch_shapes=[pltpu.VMEM((tile * ROW_TILES, LANES), F32),
                        pltpu.VMEM((3, rows * ROW_TILES, LANES), F32),
                        pltpu.SemaphoreType.DMA(()), pltpu.SemaphoreType.DMA((3,)),
                        pltpu.SemaphoreType.DMA((3,))],
    )
    return pl.pallas_call(
        functools.partial(_dispatch_body, rows=rows, tile=tile, n_tiles=m_tot // tile),
        grid_spec=grid_spec,
        out_shape=jax.ShapeDtypeStruct((m_tot * ROW_TILES, LANES), F32),
        compiler_params=_params(("arbitrary",)),
        name="dispatch",
    )(tail_start, has_rows, n_valid, pos, f3)


def _experts_body(tidx_ref, te_ref, nval_ref, x_ref, wg_ref, wu_ref, wd_ref, bg_ref, bu_ref, bd_ref,
                  y_ref, wg_s, wu_s, wd_s):
    i = pl.program_id(0)
    e = te_ref[i]
    prev = te_ref[jnp.maximum(i - 1, 0)]
    tile = x_ref.shape[0] // ROW_TILES

    @pl.when((i == 0) | (e != prev))
    def _():
        wg_s[...] = wg_ref[0].astype(BF16)
        wu_s[...] = wu_ref[0].astype(BF16)
        wd_s[...] = wd_ref[0].astype(BF16)

    @pl.when(i < nval_ref[0])
    def _():
        x = _load_rows(x_ref, tile).astype(BF16)
        g = jnp.minimum(_dot(x, wg_s[...]) + bg_ref[0], SWIGLU_LIMIT)
        up = jnp.clip(_dot(x, wu_s[...]) + bu_ref[0], -SWIGLU_LIMIT, SWIGLU_LIMIT)
        act = g * jax.nn.sigmoid(SWIGLU_ALPHA * g) * (up + 1.0)
        y = _dot(act.astype(BF16), wd_s[...]) + bd_ref[0]
        _store_rows(y_ref, y, tile)

    @pl.when(i >= nval_ref[0])
    def _():
        y_ref[...] = jnp.zeros_like(y_ref)


def _experts(tile_idx, tile_e, n_valid, xs, w_gate, w_up, w_down, b_gate, b_up, b_down, tile):
    n_tiles = tile_idx.shape[0]
    d_ff = w_gate.shape[-1]
    row = pl.BlockSpec((tile * ROW_TILES, LANES), lambda i, ti, te, nv: (ti[i], 0))
    w_spec = lambda a: pl.BlockSpec((1,) + a.shape[1:], lambda i, ti, te, nv: (te[i], 0, 0))
    b3 = lambda b: b.reshape(N_EXPERTS, 1, b.shape[-1])
    grid_spec = pltpu.PrefetchScalarGridSpec(
        num_scalar_prefetch=3,
        grid=(n_tiles,),
        in_specs=[row, w_spec(w_gate), w_spec(w_up), w_spec(w_down),
                  w_spec(b3(b_gate)), w_spec(b3(b_up)), w_spec(b3(b_down))],
        out_specs=pl.BlockSpec((tile * ROW_TILES, LANES), lambda i, ti, te, nv: (i, 0)),
        scratch_shapes=[pltpu.VMEM((D_MODEL, d_ff), BF16), pltpu.VMEM((D_MODEL, d_ff), BF16),
                        pltpu.VMEM((d_ff, D_MODEL), BF16)],
    )
    return pl.pallas_call(
        _experts_body,
        grid_spec=grid_spec,
        out_shape=jax.ShapeDtypeStruct(xs.shape, F32),
        compiler_params=_params(("arbitrary",)),
        name="experts",
    )(tile_idx, tile_e, n_valid, xs, w_gate, w_up, w_down, b3(b_gate), b3(b_up), b3(b_down))


def _combine_body(pos_ref, next_pos_ref, wts_ref, h1_ref, g_ref, y_hbm, yp_ref, ys_ref, *scratch,
                  nb, n_blk, rows):
    bufs, sem = (scratch[:TOP_K], scratch[TOP_K:2 * TOP_K]), scratch[2 * TOP_K]
    i = pl.program_id(0)

    def gather(p_ref, slot):
        def issue(r, c):
            for k in range(TOP_K):
                pltpu.make_async_copy(_tile_rows(y_hbm, p_ref[r * TOP_K + k]),
                                      _tile_rows(bufs[slot][k], r), sem.at[slot]).start(priority=k % 2)
            return c
        lax.fori_loop(0, rows, issue, 0, unroll=ISSUE_UNROLL)

    def finish(slot):
        for k in range(TOP_K):
            pltpu.make_async_copy(_tile_rows(y_hbm, 0, rows), bufs[slot][k], sem.at[slot]).wait()
        w = wts_ref[...]
        moe = sum(w[:, k:k + 1] * _load_rows(bufs[slot][k], rows) for k in range(TOP_K))
        y = _rms(h1_ref[...] + moe, g_ref[...])

        @pl.when(i < n_blk)
        def _():
            yp_ref[...] = y.reshape(nb, MERGE_T, D_MODEL)

        @pl.when(i == n_blk)
        def _():
            ys_ref[...] = y

    @pl.when(i == 0)
    def _():
        gather(pos_ref, 0)

    for slot in range(2):
        @pl.when(i % 2 == slot)
        def _(slot=slot):
            @pl.when(i < n_blk)
            def _():
                gather(next_pos_ref, 1 - slot)
            finish(slot)


def _combine(pos, wts, h1, g, ys, nb, t):
    rows = nb * MERGE_T
    n_blk = t // MERGE_T
    return pl.pallas_call(
        functools.partial(_combine_body, nb=nb, n_blk=n_blk, rows=rows),
        grid=(n_blk + 1,),
        in_specs=[pl.BlockSpec((rows * TOP_K,), lambda i: (i,), memory_space=pltpu.SMEM),
                  pl.BlockSpec((rows * TOP_K,), lambda i: (jnp.minimum(i + 1, n_blk),),
                               memory_space=pltpu.SMEM),
                  pl.BlockSpec((rows, LANES), lambda i: (i, 0)),
                  pl.BlockSpec((rows, D_MODEL), lambda i: (i, 0)),
                  pl.BlockSpec((1, D_MODEL), lambda i: (0, 0)),
                  pl.BlockSpec(memory_space=pl.ANY)],
        out_specs=[pl.BlockSpec((nb, MERGE_T, D_MODEL), lambda i: (0, jnp.minimum(i, n_blk - 1), 0)),
                   pl.BlockSpec((rows, D_MODEL), lambda i: (0, 0))],
        out_shape=[jax.ShapeDtypeStruct((nb, t, D_MODEL), F32),
                   jax.ShapeDtypeStruct((rows, D_MODEL), F32)],
        scratch_shapes=[pltpu.VMEM((rows * ROW_TILES, LANES), F32)] * (2 * TOP_K)
                       + [pltpu.SemaphoreType.DMA((2,))],
        compiler_params=_params(("arbitrary",)),
        name="combine",
    )(pos, pos, wts, h1, g, ys)


def _moe_plan(counts, idx, rank, tile):
    n_tok = idx.shape[0]
    n_tiles = (n_tok * TOP_K) // tile + N_EXPERTS
    padded = ((counts + tile - 1) // tile) * tile
    ends = jnp.cumsum(padded)
    off = ends - padded
    pos = (off[idx] + rank).reshape(-1)
    tile_ends = ends // tile
    n_valid = tile_ends[-1]
    tile_idx = jnp.minimum(jnp.arange(n_tiles, dtype=I32), n_valid - 1)
    tile_e = jnp.minimum(jnp.sum(tile_idx[:, None] >= tile_ends[None, :], axis=1), N_EXPERTS - 1).astype(I32)
    return pos.astype(I32), tile_idx, tile_e, n_valid.reshape(1).astype(I32), (ends - tile).astype(I32), \
        (counts > 0).astype(I32), n_tiles * tile


def kernel(x_prompt, x_sample, cache_k, cache_v, page_table, state_ssm_re, state_ssm_im, meta_tokens, g_mix, w_in, sb_bias, lambda_re, lambda_im, log_dt, b_re, b_im, c_re, c_im, d_skip, w_glu, b_glu, g_att_out, g_ssm_out, w_out, g_ffn, w_router, b_router, w_gate, b_gate, w_up, b_up, w_down, b_down, g_final):
    depth = g_mix.shape[0]
    assert depth == 1, "single-layer step"
    b, t, _ = x_prompt.shape
    db, ds, _ = x_sample.shape
    n_meta = meta_tokens.shape[0]
    assert ds == 1 and db == b * MERGE_T and t % BLOCK == 0 and n_meta <= BLOCK
    l = 0
    row = lambda a: a.reshape(1, -1)

    w_in_b = w_in[l].astype(BF16)
    g_mix_l = row(g_mix[l])
    q_p, k_p, v_p, kb_p, vb_p, u_p = _project(x_prompt, g_mix_l, w_in_b, PROJ_T)
    _, k_m, v_m, kb_m, vb_m, u_m = _project(meta_tokens[None], g_mix_l, w_in_b, n_meta)
    q_s, k_s, v_s, _, _, u_s = _project(x_sample.reshape(1, db, D_MODEL), g_mix_l, w_in_b, db)

    att_p = _attn_prompt(sb_bias[l], q_p, kb_p, vb_p, kb_m[0], vb_m[0], n_meta)
    n_phys, page = cache_k.shape[1], cache_k.shape[2]
    head_of_lane = jnp.arange(ATT_WIDTH) // HEAD_DIM
    q_bd = jnp.where(head_of_lane[None, None, :] == jnp.arange(N_HEADS)[None, :, None],
                     q_s[0][:, None, :], jnp.zeros((), BF16))
    keys_minor = lambda c: jnp.transpose(c, (0, 2, 3, 1)).reshape(n_phys, ATT_WIDTH, page)
    att_s = _attn_sample(page_table, q_bd, sb_bias[l].reshape(N_HEADS, 1),
                         keys_minor(cache_k[l]), keys_minor(cache_v[l]))

    ab_re, ab_im, bb_re, bb_im = _s5_prep(lambda_re[l], lambda_im[l], log_dt[l], b_re[l], b_im[l])
    bbd = jnp.concatenate([_block_diag_in(bb_re), _block_diag_in(bb_im)], axis=2)
    cre = _block_diag_out(c_re[l]).astype(BF16)
    cim = _block_diag_out(c_im[l]).astype(BF16)
    a_re = ab_re.reshape(2, 1, HALF_STATE)
    a_im = ab_im.reshape(2, 1, HALF_STATE)
    s5_w = (cre, cim, a_re, a_im, row(d_skip[l]), w_glu[l].astype(BF16), row(b_glu[l]), row(g_ssm_out[l]))
    zero_state = jnp.zeros((2, 2, b, HALF_STATE), F32)
    s_p, hf_p = _s5(u_p, u_m.reshape(n_meta, SSM_WIDTH), zero_state, bbd.astype(BF16), *s5_w)
    h0_s = jnp.stack([_state_in(state_ssm_re[l]), _state_in(state_ssm_im[l])], axis=1)
    s_s, hf_s = _s5(u_s.reshape(1, db, SSM_WIDTH), None, h0_s, bbd, *s5_w)

    h1, f3, idx, wts, rank, cnt = _merge(
        x_prompt, att_p, s_p, x_sample.reshape(db, D_MODEL), att_s.reshape(db, ATT_WIDTH),
        s_s.reshape(db, SSM_WIDTH), row(g_att_out[l]), w_out[l].astype(BF16), row(g_ffn[l]),
        w_router[l], row(b_router[l]))
    counts = cnt[0].astype(I32)
    pos, tile_idx, tile_e, n_valid, tail_start, has_rows, m_tot = _moe_plan(
        counts, idx[:, :TOP_K], rank[:, :TOP_K], EXPERT_TILE)
    rows = b * MERGE_T
    xs = _dispatch(tail_start, has_rows, n_valid, pos, f3, rows, m_tot, EXPERT_TILE)
    ys = _experts(tile_idx, tile_e, n_valid, xs, w_gate[l], w_up[l], w_down[l],
                  b_gate[l], b_up[l], b_down[l], EXPERT_TILE)
    y_prompt, y_sample = _combine(pos, wts, h1, row(g_final), ys, b, t)

    heads = lambda a: a.reshape(a.shape[:-1] + (N_HEADS, HEAD_DIM))
    with_meta = lambda m, p: heads(jnp.concatenate([jnp.broadcast_to(m, (b,) + m.shape[1:]), p], axis=1))[None]
    return (y_prompt, y_sample.reshape(db, ds, D_MODEL),
            with_meta(k_m, k_p), with_meta(v_m, v_p),
            heads(k_s.reshape(db, ds, ATT_WIDTH))[None], heads(v_s.reshape(db, ds, ATT_WIDTH))[None],
            _state_out(hf_p[:, 0])[None], _state_out(hf_p[:, 1])[None],
            _state_out(hf_s[:, 0])[None], _state_out(hf_s[:, 1])[None])
```

```python
import functools
import math

import jax
import jax.numpy as jnp
from jax import lax
from jax.experimental import pallas as pl
from jax.experimental.pallas import tpu as pltpu

F32 = jnp.float32
BF16 = jnp.bfloat16
I32 = jnp.int32

D_MODEL = 1024
ATT_WIDTH = 512
N_HEADS = 8
HEAD_DIM = 64
SSM_WIDTH = 512
SSM_GROUP = 16
N_GROUPS = 32
STATE_DIM = 64
N_EXPERTS = 32
TOP_K = 4
SWIGLU_LIMIT = 7.0
SWIGLU_ALPHA = 1.702
EPS = 1e-5
ATT_SCALE = 1.0 / math.sqrt(HEAD_DIM)
BLOCK = 128

LANES = 128
SUBLANES = 8
ROW_TILES = D_MODEL // LANES
HALF_GROUPS = N_GROUPS // 2
HALF_IN = HALF_GROUPS * SSM_GROUP
HALF_STATE = HALF_GROUPS * STATE_DIM
VMEM_LIMIT = 56 * 1024 * 1024

PROJ_T = 32
S5_T = 32
MERGE_T = SUBLANES
PAGES_PER_STEP = 16
EXPERT_TILE = 512
ISSUE_UNROLL = 4
GROUP_UNROLL = 8


def _rms(x, g):
    return x * lax.rsqrt(jnp.mean(x * x, axis=-1, keepdims=True) + EPS) * g


def _dot(a, b, **kw):
    return jnp.dot(a, b, preferred_element_type=F32, **kw)


def _dot_nt(a, b):
    return lax.dot_general(a, b, (((1,), (1,)), ((), ())), preferred_element_type=F32)


def _params(sem=None):
    return pltpu.CompilerParams(dimension_semantics=sem, vmem_limit_bytes=VMEM_LIMIT)


def _tile_rows(ref, r, n=1):
    return ref.at[pl.ds(pl.multiple_of(r * ROW_TILES, ROW_TILES), n * ROW_TILES)]


def _tile_at(ref, first):
    return ref.at[pl.ds(pl.multiple_of(first, ROW_TILES), ROW_TILES)]


def _load_rows(ref, n):
    return jnp.concatenate([ref[pl.ds(c, n, stride=ROW_TILES), :] for c in range(ROW_TILES)], axis=1)


def _store_rows(ref, val, n):
    for c in range(ROW_TILES):
        ref[pl.ds(c, n, stride=ROW_TILES), :] = val[:, c * LANES:(c + 1) * LANES]


def _proj_body(x_ref, g_ref, w_ref, q_ref, k_ref, v_ref, kb_ref, vb_ref, u_ref, *, nb, tt):
    x = x_ref[...].reshape(nb * tt, D_MODEL)
    h = _rms(x, g_ref[...]).astype(BF16)
    p = _dot(h, w_ref[...])
    q = p[:, :ATT_WIDTH] * ATT_SCALE
    k = p[:, ATT_WIDTH:2 * ATT_WIDTH]
    v = p[:, 2 * ATT_WIDTH:3 * ATT_WIDTH]
    u = p[:, 3 * ATT_WIDTH:]
    q_ref[...] = q.astype(BF16).reshape(nb, tt, ATT_WIDTH)
    k_ref[...] = k.reshape(nb, tt, ATT_WIDTH)
    v_ref[...] = v.reshape(nb, tt, ATT_WIDTH)
    kb_ref[...] = k.astype(BF16).reshape(nb, tt, ATT_WIDTH)
    vb_ref[...] = v.astype(BF16).reshape(nb, tt, ATT_WIDTH)
    for b in range(nb):
        u_ref[:, b, :] = u[b * tt:(b + 1) * tt, :]


def _project(x, g, w, tt):
    nb, t, _ = x.shape
    spec_x = pl.BlockSpec((nb, tt, D_MODEL), lambda i: (0, i, 0))
    spec_a = pl.BlockSpec((nb, tt, ATT_WIDTH), lambda i: (0, i, 0))
    shp = lambda dt: jax.ShapeDtypeStruct((nb, t, ATT_WIDTH), dt)
    return pl.pallas_call(
        functools.partial(_proj_body, nb=nb, tt=tt),
        grid=(t // tt,),
        in_specs=[spec_x,
                  pl.BlockSpec((1, D_MODEL), lambda i: (0, 0)),
                  pl.BlockSpec(w.shape, lambda i: (0, 0))],
        out_specs=[spec_a, spec_a, spec_a, spec_a, spec_a,
                   pl.BlockSpec((tt, nb, SSM_WIDTH), lambda i: (i, 0, 0))],
        out_shape=[shp(BF16), shp(F32), shp(F32), shp(BF16), shp(BF16),
                   jax.ShapeDtypeStruct((t, nb, SSM_WIDTH), F32)],
        compiler_params=_params(("arbitrary",)),
        name="proj",
    )(x, g, w)


LOG2E = 1.0 / math.log(2.0)


def _softplus(z):
    return jnp.maximum(z, 0.0) + jnp.log2(1.0 + jnp.exp2(-jnp.abs(z)))


def _sb_split(z, valid):
    sp = _softplus(z)
    if valid is not None:
        sp = jnp.where(valid, sp, 0.0)
    hi = sp.astype(BF16)
    lo = (sp - hi.astype(F32)).astype(BF16)
    return jnp.concatenate([hi, lo], axis=1)


def _sb_weights(z, ext, valid, carry):
    a = jnp.exp2(z - ext[:, :BLOCK] - carry)
    if valid is not None:
        a = jnp.where(valid, a, 0.0)
    return a.astype(BF16), carry + ext[:, BLOCK:]


def _suffix_matrix():
    j = lax.broadcasted_iota(I32, (2 * BLOCK, 2 * BLOCK), 0)
    s = lax.broadcasted_iota(I32, (2 * BLOCK, 2 * BLOCK), 1)
    j = jnp.where(j >= BLOCK, j - BLOCK, j)
    return ((s >= BLOCK) | (j >= s)).astype(BF16)


def _attn_prompt_body(bias_ref, q_ref, k_ref, v_ref, km_ref, vm_ref, bm_ref, o_ref, qm_ref, carry_ref, acc_ref,
                      *, n_meta):
    jq = pl.program_id(1)
    m2 = _suffix_matrix()
    row = lax.broadcasted_iota(I32, (BLOCK, BLOCK), 0)
    col = lax.broadcasted_iota(I32, (BLOCK, LANES), 1)
    causal = col < row
    low_half = col < HEAD_DIM
    pair_lanes = lambda p: slice(p * LANES, (p + 1) * LANES)

    for h in range(N_HEADS):
        q_pair = q_ref[0, :, pair_lanes(h // 2)]
        qm_ref[h] = jnp.where(low_half == (h % 2 == 0), q_pair, jnp.zeros_like(q_pair))
    carry_ref[...] = jnp.zeros_like(carry_ref)
    acc_ref[...] = jnp.zeros_like(acc_ref)

    def blocks(sources, valid):
        heads = range(N_HEADS)
        zs = [[(_dot_nt(qm_ref[h], k_of(pair_lanes(h // 2))) + bias_ref[h]) * LOG2E for h in heads]
              for k_of, _ in sources]
        exts = [[_dot(_sb_split(z, valid), m2) for z in zb] for zb in zs]
        pv = []
        for h in heads:
            carry, out = carry_ref[h], None
            for s, (_, v_of) in enumerate(sources):
                a, carry = _sb_weights(zs[s][h], exts[s][h], valid, carry)
                d = _dot(a, v_of(pair_lanes(h // 2)))
                out = d if out is None else out + d
            carry_ref[h] = carry
            pv.append(out)
        for p in range(N_HEADS // 2):
            acc_ref[:, pair_lanes(p)] += jnp.where(low_half, pv[2 * p], pv[2 * p + 1])

    def key_block(j):
        rows = pl.ds(pl.multiple_of(j * BLOCK, BLOCK), BLOCK)
        return (lambda ls: k_ref[0, rows, ls]), (lambda ls: v_ref[0, rows, ls])

    blocks([key_block(jq)], causal)

    def older_pair(i, c):
        blocks([key_block(jq - 1 - 2 * i), key_block(jq - 2 - 2 * i)], None)
        return c

    lax.fori_loop(0, jq // 2, older_pair, 0)

    @pl.when(jq % 2 == 1)
    def _():
        blocks([key_block(0)], None)

    zm = (_dot(q_ref[0], km_ref[...]) + bm_ref[...]) * LOG2E
    mj = lax.broadcasted_iota(I32, (2 * LANES, LANES), 0) % LANES
    ms = lax.broadcasted_iota(I32, (2 * LANES, LANES), 1)
    same_head_suffix = ((mj // n_meta == ms // n_meta) & (mj >= ms)).astype(BF16)
    suffix = _dot(_sb_split(zm, None), same_head_suffix)
    carry = jnp.zeros((BLOCK, LANES), F32)
    for h in range(N_HEADS):
        carry = jnp.where(col // n_meta == h, carry_ref[h], carry)
    am = jnp.exp2(zm - suffix - carry).astype(BF16)
    o_ref[0] = acc_ref[...] + _dot(am, vm_ref[...])


def _attn_prompt(bias, q, k, v, km, vm, n_meta):
    b, t, _ = q.shape
    assert n_meta * N_HEADS == LANES
    eye = jnp.eye(N_HEADS, dtype=km.dtype)
    per_head = lambda a: a.reshape(n_meta, N_HEADS, HEAD_DIM)
    km = jnp.einsum('shd,hg->hdgs', per_head(km), eye).reshape(ATT_WIDTH, LANES)
    vm = jnp.einsum('shd,hg->gshd', per_head(vm), eye).reshape(LANES, ATT_WIDTH)
    bm = jnp.repeat(bias, n_meta).reshape(1, LANES)
    return pl.pallas_call(
        functools.partial(_attn_prompt_body, n_meta=n_meta),
        grid=(b, t // BLOCK),
        in_specs=[pl.BlockSpec(memory_space=pltpu.SMEM),
                  pl.BlockSpec((1, BLOCK, ATT_WIDTH), lambda i, j: (i, j, 0)),
                  pl.BlockSpec((1, t, ATT_WIDTH), lambda i, j: (i, 0, 0)),
                  pl.BlockSpec((1, t, ATT_WIDTH), lambda i, j: (i, 0, 0)),
                  pl.BlockSpec((ATT_WIDTH, LANES), lambda i, j: (0, 0)),
                  pl.BlockSpec((LANES, ATT_WIDTH), lambda i, j: (0, 0)),
                  pl.BlockSpec((1, LANES), lambda i, j: (0, 0))],
        out_specs=pl.BlockSpec((1, BLOCK, ATT_WIDTH), lambda i, j: (i, j, 0)),
        out_shape=jax.ShapeDtypeStruct((b, t, ATT_WIDTH), F32),
        scratch_shapes=[pltpu.VMEM((N_HEADS, BLOCK, LANES), BF16),
                        pltpu.VMEM((N_HEADS, BLOCK, LANES), F32),
                        pltpu.VMEM((BLOCK, ATT_WIDTH), F32)],
        compiler_params=_params(("arbitrary", "arbitrary")),
        name="attn_prompt",
    )(bias, q, k, v, km, vm, bm)


def _attn_sample_body(pt_ref, q_ref, bias_ref, k_hbm, v_hbm, o_ref, carry_ref, acc_ref, kbuf, vbuf, sem,
                      *, pps, n_pages):
    b, c = pl.program_id(0), pl.program_id(1)
    n_chunks = pl.num_programs(1)
    step = b * n_chunks + c
    slot = step % 2

    def fetch(step, slot):
        sb, sc = step // n_chunks, step % n_chunks
        base = sb * n_pages + (n_chunks - 1 - sc) * pps
        for i in range(pps):
            page = pt_ref[base + i]
            pltpu.make_async_copy(k_hbm.at[page], kbuf.at[slot, i], sem.at[0, slot]).start()
            pltpu.make_async_copy(v_hbm.at[page], vbuf.at[slot, i], sem.at[1, slot]).start()

    @pl.when(step == 0)
    def _():
        fetch(step, slot)

    @pl.when(step + 1 < pl.num_programs(0) * n_chunks)
    def _():
        fetch(step + 1, 1 - slot)

    @pl.when(c == 0)
    def _():
        carry_ref[...] = jnp.zeros_like(carry_ref)
        acc_ref[...] = jnp.zeros_like(acc_ref)

    pltpu.make_async_copy(k_hbm.at[pl.ds(0, pps)], kbuf.at[slot], sem.at[0, slot]).wait()
    pltpu.make_async_copy(v_hbm.at[pl.ds(0, pps)], vbuf.at[slot], sem.at[1, slot]).wait()

    q = q_ref[0]
    order = list(reversed(range(pps)))
    z = jnp.concatenate([(_dot(q, kbuf[slot, i].astype(BF16)) + bias_ref[...]) * LOG2E for i in order], axis=0)
    ext = _dot(_sb_split(z, None), _suffix_matrix())
    carry = carry_ref[...]
    carries = []
    for j in range(pps):
        carries.append(carry)
        carry = carry + ext[j * N_HEADS:(j + 1) * N_HEADS, BLOCK:]
    a = jnp.exp2(z - ext[:, :BLOCK] - jnp.concatenate(carries, axis=0))
    acc = acc_ref[...]
    for j, i in enumerate(order):
        a_j = a[j * N_HEADS:(j + 1) * N_HEADS].astype(BF16)
        acc = acc + _dot_nt(a_j, vbuf[slot, i].astype(BF16))
    carry_ref[...] = carry
    acc_ref[...] = acc

    @pl.when(c == n_chunks - 1)
    def _():
        head = lax.broadcasted_iota(I32, acc.shape, 0)
        lane = lax.broadcasted_iota(I32, acc.shape, 1)
        own = lane // HEAD_DIM == head
        o_ref[0] = jnp.sum(jnp.where(own, acc, 0.0), axis=0, keepdims=True)


def _attn_sample(page_table, q_bd, bias, cache_k, cache_v):
    db, n_pages = page_table.shape
    page = cache_k.shape[2]
    assert page == BLOCK
    pps = min(PAGES_PER_STEP, n_pages)
    n_chunks = n_pages // pps
    grid_spec = pltpu.PrefetchScalarGridSpec(
        num_scalar_prefetch=1,
        grid=(db, n_chunks),
        in_specs=[pl.BlockSpec((1, N_HEADS, ATT_WIDTH), lambda b, c, pt: (b, 0, 0)),
                  pl.BlockSpec((N_HEADS, 1), lambda b, c, pt: (0, 0)),
                  pl.BlockSpec(memory_space=pl.ANY), pl.BlockSpec(memory_space=pl.ANY)],
        out_specs=pl.BlockSpec((1, 1, ATT_WIDTH), lambda b, c, pt: (b, 0, 0)),
        scratch_shapes=[pltpu.VMEM((N_HEADS, LANES), F32), pltpu.VMEM((N_HEADS, ATT_WIDTH), F32),
                        pltpu.VMEM((2, pps, ATT_WIDTH, page), F32), pltpu.VMEM((2, pps, ATT_WIDTH, page), F32),
                        pltpu.SemaphoreType.DMA((2, 2))],
    )
    return pl.pallas_call(
        functools.partial(_attn_sample_body, pps=pps, n_pages=n_pages),
        grid_spec=grid_spec,
        out_shape=jax.ShapeDtypeStruct((db, 1, ATT_WIDTH), F32),
        compiler_params=_params(("arbitrary", "arbitrary")),
        name="attn_sample",
    )(page_table.reshape(-1), q_bd, bias, cache_k, cache_v)


def _s5_prep_body(lr_ref, li_ref, ldt_ref, br_ref, bi_ref, ar_ref, ai_ref, bbr_ref, bbi_ref):
    lr, li = lr_ref[...], li_ref[...]
    dt = jnp.exp(ldt_ref[...])
    mag = jnp.exp(lr * dt)
    ang = li * dt
    ab_re, ab_im = mag * jnp.cos(ang), mag * jnp.sin(ang)
    den = lr * lr + li * li
    num_re = ab_re - 1.0
    coef_re = (num_re * lr + ab_im * li) / den
    coef_im = (ab_im * lr - num_re * li) / den
    br, bi = br_ref[...], bi_ref[...]
    ar_ref[...] = ab_re
    ai_ref[...] = ab_im
    bbr_ref[...] = coef_re * br - coef_im * bi
    bbi_ref[...] = coef_re * bi + coef_im * br


def _s5_prep(lam_re, lam_im, log_dt, b_re, b_im):
    g, n = lam_re.shape
    v3 = lambda a: a.reshape(g, 1, n)
    bt = lambda a: jnp.swapaxes(a, 1, 2)
    small = jax.ShapeDtypeStruct((g, 1, n), F32)
    big = jax.ShapeDtypeStruct((g, SSM_GROUP, n), F32)
    return pl.pallas_call(
        _s5_prep_body, out_shape=[small, small, big, big], name="s5_prep",
    )(v3(lam_re), v3(lam_im), jnp.broadcast_to(log_dt[:, None, None], (g, 1, n)), bt(b_re), bt(b_im))


def _s5_body(u_ref, um_ref, h0_ref, bbd_ref, cre_ref, cim_ref, are_ref, aim_ref, d_ref,
             wglu_ref, bglu_ref, g_ref, s_ref, hf_ref, st_ref, bu_ref, *, tc, nb, n_meta, lane_w):
    i = pl.program_id(0)
    rows = tc * nb
    precision = lax.Precision.HIGHEST if bbd_ref.dtype == F32 else None

    def drive(uh, h):
        return _dot(uh.astype(bbd_ref.dtype), bbd_ref[h], precision=precision)

    @pl.when(i == 0)
    def _():
        st_ref[...] = h0_ref[...]
        for h in range(2 if n_meta else 0):
            bum = drive(um_ref[:, h * HALF_IN:(h + 1) * HALF_IN], h)
            ar, ai = are_ref[h], aim_ref[h]
            xr, xi = st_ref[h, 0], st_ref[h, 1]
            for t in range(n_meta):
                br, bi = bum[t:t + 1, :HALF_STATE], bum[t:t + 1, HALF_STATE:]
                xr, xi = ar * xr - ai * xi + br, ar * xi + ai * xr + bi
            st_ref[h, 0] = xr
            st_ref[h, 1] = xi

    u = u_ref[...].reshape(rows, SSM_WIDTH)
    ys = []
    for h in range(2):
        bu = drive(u[:, h * HALF_IN:(h + 1) * HALF_IN], h)
        bu_ref[0] = bu[:, :HALF_STATE]
        bu_ref[1] = bu[:, HALF_STATE:]
        for lc in range(HALF_STATE // lane_w):
            sl = slice(lc * lane_w, (lc + 1) * lane_w)
            ar = jnp.broadcast_to(are_ref[h, :, sl], (nb, lane_w))
            ai = jnp.broadcast_to(aim_ref[h, :, sl], (nb, lane_w))

            def step(t, x, sl=sl, ar=ar, ai=ai):
                xr, xi = x
                r0 = pl.multiple_of(t * nb, nb)
                nxr = ar * xr - ai * xi + bu_ref[0, pl.ds(r0, nb), sl]
                nxi = ar * xi + ai * xr + bu_ref[1, pl.ds(r0, nb), sl]
                bu_ref[0, pl.ds(r0, nb), sl] = nxr
                bu_ref[1, pl.ds(r0, nb), sl] = nxi
                return nxr, nxi

            xr, xi = lax.fori_loop(0, tc, step, (st_ref[h, 0, :, sl], st_ref[h, 1, :, sl]))
            st_ref[h, 0, :, sl] = xr
            st_ref[h, 1, :, sl] = xi
        ys.append(_dot(bu_ref[0].astype(BF16), cre_ref[h]) - _dot(bu_ref[1].astype(BF16), cim_ref[h]))
    y = jnp.concatenate(ys, axis=1) + d_ref[...] * u
    y = jax.nn.gelu(y)
    y = y * jax.nn.sigmoid(_dot(y.astype(BF16), wglu_ref[...]) + bglu_ref[...])
    s_ref[...] = _rms(y, g_ref[...]).reshape(tc, nb, SSM_WIDTH)

    @pl.when(i == pl.num_programs(0) - 1)
    def _():
        hf_ref[...] = st_ref[...]


def _s5(u, u_meta, h0, bbd, cre, cim, a_re, a_im, d, w_glu, b_glu, g):
    t, nb, _ = u.shape
    tc = min(S5_T, t)
    n_meta = 0 if u_meta is None else u_meta.shape[0]
    if u_meta is None:
        u_meta = jnp.zeros((SUBLANES, SSM_WIDTH), F32)
    lane_w = max(LANES, min(4 * LANES, (64 * LANES) // nb))
    full = lambda a: pl.BlockSpec(a.shape, lambda i: (0,) * a.ndim)
    blk = pl.BlockSpec((tc, nb, SSM_WIDTH), lambda i: (i, 0, 0))
    args = (u, u_meta, h0, bbd, cre, cim, a_re, a_im, d, w_glu, b_glu, g)
    return pl.pallas_call(
        functools.partial(_s5_body, tc=tc, nb=nb, n_meta=n_meta, lane_w=lane_w),
        grid=(t // tc,),
        in_specs=[blk] + [full(a) for a in args[1:]],
        out_specs=[blk, full(h0)],
        out_shape=[jax.ShapeDtypeStruct(u.shape, F32), jax.ShapeDtypeStruct(h0.shape, F32)],
        scratch_shapes=[pltpu.VMEM(h0.shape, F32), pltpu.VMEM((2, tc * nb, HALF_STATE), F32)],
        compiler_params=_params(("arbitrary",)),
        name="s5",
    )(*args)


def _block_diag_in(bb):
    eye = jnp.eye(HALF_GROUPS, dtype=F32)
    bb = bb.reshape(2, HALF_GROUPS, SSM_GROUP, STATE_DIM)
    return jnp.einsum('hgcn,gk->hgckn', bb, eye).reshape(2, HALF_IN, HALF_STATE)


def _block_diag_out(c):
    eye = jnp.eye(HALF_GROUPS, dtype=F32)
    c = c.reshape(2, HALF_GROUPS, SSM_GROUP, STATE_DIM)
    return jnp.einsum('hgcn,kg->hkngc', c, eye).reshape(2, HALF_STATE, HALF_IN)


def _state_in(s):
    nb = s.shape[0]
    return jnp.swapaxes(s.reshape(nb, 2, HALF_STATE), 0, 1)


def _state_out(s):
    nb = s.shape[1]
    return jnp.swapaxes(s, 0, 1).reshape(nb, N_GROUPS, STATE_DIM)


def _merge_body(xp_ref, ap_ref, sp_ref, xs_ref, as_ref, ss_ref, gatt_ref, wout_ref, gffn_ref,
                wr_ref, br_ref, h1_ref, f3_ref, idx_ref, wts_ref, rank_ref, cnt_ref, run_ref, logit_ref,
                *, nb, n_blk):
    i = pl.program_id(0)
    rows = nb * MERGE_T
    is_sample = i >= n_blk
    slot = i % 2

    @pl.when(i == 0)
    def _():
        run_ref[...] = jnp.zeros_like(run_ref)
        logit_ref[...] = jnp.zeros_like(logit_ref)

    s_bt = jnp.concatenate([sp_ref[:, b, :] for b in range(nb)], axis=0)
    x = jnp.where(is_sample, xs_ref[...], xp_ref[...].reshape(rows, D_MODEL))
    att = jnp.where(is_sample, as_ref[...], ap_ref[...].reshape(rows, ATT_WIDTH))
    ssm = jnp.where(is_sample, ss_ref[...], s_bt)
    cat = jnp.concatenate([_rms(att, gatt_ref[...]), ssm], axis=1).astype(BF16)
    h1 = x + _dot(cat, wout_ref[...])
    f = _rms(h1, gffn_ref[...])
    h1_ref[...] = h1
    _store_rows(f3_ref, f, rows)

    logits = logit_ref[1 - slot]
    logit_ref[slot] = _dot(f, wr_ref[...], precision=lax.Precision.HIGHEST) + br_ref[...]
    routed = (i >= 1).astype(F32)
    lane = lax.broadcasted_iota(I32, logits.shape, 1)
    vals, ids, hots = [], [], []
    for _ in range(TOP_K):
        m = jnp.max(logits, axis=1, keepdims=True)
        sel = jnp.min(jnp.where(logits == m, lane, N_EXPERTS), axis=1, keepdims=True)
        hot = lane == sel
        vals.append(m)
        ids.append(sel)
        hots.append(hot)
        logits = jnp.where(hot, -jnp.inf, logits)
    es = [jnp.exp(v - vals[0]) for v in vals]
    den = es[0] + es[1] + es[2] + es[3]

    chosen = (hots[0] | hots[1] | hots[2] | hots[3])
    tr = lax.broadcasted_iota(I32, (rows, rows), 0)
    tc = lax.broadcasted_iota(I32, (rows, rows), 1)
    before = _dot((tc < tr).astype(BF16), chosen.astype(BF16)) + run_ref[...]
    run_ref[...] = run_ref[...] + routed * jnp.sum(chosen.astype(F32), axis=0, keepdims=True)
    cnt_ref[...] = run_ref[...]

    out_lane = lax.broadcasted_iota(I32, (rows, LANES), 1)
    idx_o = jnp.zeros((rows, LANES), I32)
    wts_o = jnp.zeros((rows, LANES), F32)
    rank_o = jnp.zeros((rows, LANES), I32)
    for k in range(TOP_K):
        rank_k = jnp.sum(jnp.where(hots[k], before, 0.0), axis=1, keepdims=True).astype(I32)
        idx_o = jnp.where(out_lane == k, ids[k], idx_o)
        wts_o = jnp.where(out_lane == k, es[k] / den, wts_o)
        rank_o = jnp.where(out_lane == k, rank_k, rank_o)
    idx_ref[...] = idx_o
    wts_ref[...] = wts_o
    rank_ref[...] = rank_o


def _merge(x_p, att_p, s_p, x_s, att_s, s_s, g_att, w_out, g_ffn, w_r, b_r):
    nb, t, _ = x_p.shape
    rows = nb * MERGE_T
    n_blk = t // MERGE_T
    n_tok = (n_blk + 1) * rows
    pj = lambda i: jnp.minimum(i, n_blk - 1)
    full = lambda a: pl.BlockSpec(a.shape, lambda i: (0,) * a.ndim)
    merged = lambda i: jnp.minimum(i, n_blk)
    routed = lambda i: jnp.maximum(i - 1, 0)
    row_spec = lambda w: pl.BlockSpec((rows, w), lambda i: (routed(i), 0))
    return pl.pallas_call(
        functools.partial(_merge_body, nb=nb, n_blk=n_blk),
        grid=(n_blk + 2,),
        in_specs=[pl.BlockSpec((nb, MERGE_T, D_MODEL), lambda i: (0, pj(i), 0)),
                  pl.BlockSpec((nb, MERGE_T, ATT_WIDTH), lambda i: (0, pj(i), 0)),
                  pl.BlockSpec((MERGE_T, nb, SSM_WIDTH), lambda i: (pj(i), 0, 0)),
                  full(x_s), full(att_s), full(s_s), full(g_att), full(w_out), full(g_ffn),
                  full(w_r), full(b_r)],
        out_specs=[pl.BlockSpec((rows, D_MODEL), lambda i: (merged(i), 0)),
                   pl.BlockSpec((rows * ROW_TILES, LANES), lambda i: (merged(i), 0)),
                   row_spec(LANES), row_spec(LANES), row_spec(LANES),
                   pl.BlockSpec((1, N_EXPERTS), lambda i: (0, 0))],
        out_shape=[jax.ShapeDtypeStruct((n_tok, D_MODEL), F32),
                   jax.ShapeDtypeStruct((n_tok * ROW_TILES, LANES), F32),
                   jax.ShapeDtypeStruct((n_tok, LANES), I32),
                   jax.ShapeDtypeStruct((n_tok, LANES), F32),
                   jax.ShapeDtypeStruct((n_tok, LANES), I32),
                   jax.ShapeDtypeStruct((1, N_EXPERTS), F32)],
        scratch_shapes=[pltpu.VMEM((1, N_EXPERTS), F32), pltpu.VMEM((2, rows, N_EXPERTS), F32)],
        compiler_params=_params(("arbitrary",)),
        name="merge_route",
    )(x_p, att_p, s_p, x_s, att_s, s_s, g_att, w_out, g_ffn, w_r, b_r)


def _dispatch_body(tail_ref, has_ref, nval_ref, pos_ref, f3_hbm, o_hbm, zero_ref, buf_ref, sem, load_sem,
                   scatter_sem, *, rows, tile, n_tiles):
    i = pl.program_id(0)
    n = pl.num_programs(0)
    slot = i % 3

    def load(blk):
        s = blk % 3
        return pltpu.make_async_copy(_tile_rows(f3_hbm, blk * rows, rows), buf_ref.at[s], load_sem.at[s])

    def wait_scatter(s):
        for _ in range(TOP_K):
            pltpu.make_async_copy(buf_ref.at[s], _tile_rows(o_hbm, 0, rows), scatter_sem.at[s]).wait()

    def tile_copy(start):
        return pltpu.make_async_copy(zero_ref, _tile_rows(o_hbm, start, tile), sem)

    @pl.when(i == 0)
    def _():
        zero_ref[...] = jnp.zeros_like(zero_ref)

        def start(e, c):
            @pl.when(has_ref[e] > 0)
            def _():
                tile_copy(tail_ref[e]).start()
            return c

        def wait(e, c):
            @pl.when(has_ref[e] > 0)
            def _():
                tile_copy(tail_ref[e]).wait()
            return c

        def start_unused(j, c):
            tile_copy(pl.multiple_of(j * tile, tile)).start()
            return c

        def wait_unused(j, c):
            tile_copy(pl.multiple_of(j * tile, tile)).wait()
            return c

        lax.fori_loop(0, N_EXPERTS, start, 0)
        lax.fori_loop(nval_ref[0], n_tiles, start_unused, 0)
        lax.fori_loop(0, N_EXPERTS, wait, 0)
        lax.fori_loop(nval_ref[0], n_tiles, wait_unused, 0)
        load(i).start()

    @pl.when(i + 1 < n)
    def _():
        load(i + 1).start()

    load(i).wait()
    src = buf_ref.at[slot]

    def issue(r, c):
        for k in range(TOP_K):
            pltpu.make_async_copy(_tile_rows(src, r), _tile_at(o_hbm, pos_ref[r * TOP_K + k]),
                                  scatter_sem.at[slot]).start(priority=k % 2)
        return c

    lax.fori_loop(0, rows, issue, 0, unroll=ISSUE_UNROLL)

    @pl.when(i >= 1)
    def _():
        wait_scatter((i + 2) % 3)

    @pl.when(i == n - 1)
    def _():
        wait_scatter(slot)


def _dispatch(tail_start, has_rows, n_valid, pos, f3, rows, m_tot, tile):
    n_tok = f3.shape[0] // ROW_TILES
    grid_spec = pltpu.PrefetchScalarGridSpec(
        num_scalar_prefetch=3,
        grid=(n_tok // rows,),
        in_specs=[pl.BlockSpec((rows * TOP_K,), lambda i, *_: (i,), memory_space=pltpu.SMEM),
                  pl.BlockSpec(memory_space=pl.ANY)],
        out_specs=pl.BlockSpec(memory_space=pl.ANY),
        scratch_shapes=[pltpu.VMEM((tile * ROW_TILES, LANES), F32),
                        pltpu.VMEM((3, rows * ROW_TILES, LANES), F32),
                        pltpu.SemaphoreType.DMA(()), pltpu.SemaphoreType.DMA((3,)),
                        pltpu.SemaphoreType.DMA((3,))],
    )
    return pl.pallas_call(
        functools.partial(_dispatch_body, rows=rows, tile=tile, n_tiles=m_tot // tile),
        grid_spec=grid_spec,
        out_shape=jax.ShapeDtypeStruct((m_tot * ROW_TILES, LANES), F32),
        compiler_params=_params(("arbitrary",)),
        name="dispatch",
    )(tail_start, has_rows, n_valid, pos, f3)


def _experts_body(tidx_ref, te_ref, nval_ref, x_ref, wg_ref, wu_ref, wd_ref, bg_ref, bu_ref, bd_ref,
                  y_ref, wg_s, wu_s, wd_s):
    i = pl.program_id(0)
    e = te_ref[i]
    prev = te_ref[jnp.maximum(i - 1, 0)]
    tile = x_ref.shape[0] // ROW_TILES

    @pl.when((i == 0) | (e != prev))
    def _():
        wg_s[...] = wg_ref[0].astype(BF16)
        wu_s[...] = wu_ref[0].astype(BF16)
        wd_s[...] = wd_ref[0].astype(BF16)

    @pl.when(i < nval_ref[0])
    def _():
        x = _load_rows(x_ref, tile).astype(BF16)
        g = jnp.minimum(_dot(x, wg_s[...]) + bg_ref[0], SWIGLU_LIMIT)
        up = jnp.clip(_dot(x, wu_s[...]) + bu_ref[0], -SWIGLU_LIMIT, SWIGLU_LIMIT)
        act = g * jax.nn.sigmoid(SWIGLU_ALPHA * g) * (up + 1.0)
        y = _dot(act.astype(BF16), wd_s[...]) + bd_ref[0]
        _store_rows(y_ref, y, tile)

    @pl.when(i >= nval_ref[0])
    def _():
        y_ref[...] = jnp.zeros_like(y_ref)


def _experts(tile_idx, tile_e, n_valid, xs, w_gate, w_up, w_down, b_gate, b_up, b_down, tile):
    n_tiles = tile_idx.shape[0]
    d_ff = w_gate.shape[-1]
    row = pl.BlockSpec((tile * ROW_TILES, LANES), lambda i, ti, te, nv: (ti[i], 0))
    w_spec = lambda a: pl.BlockSpec((1,) + a.shape[1:], lambda i, ti, te, nv: (te[i], 0, 0))
    b3 = lambda b: b.reshape(N_EXPERTS, 1, b.shape[-1])
    grid_spec = pltpu.PrefetchScalarGridSpec(
        num_scalar_prefetch=3,
        grid=(n_tiles,),
        in_specs=[row, w_spec(w_gate), w_spec(w_up), w_spec(w_down),
                  w_spec(b3(b_gate)), w_spec(b3(b_up)), w_spec(b3(b_down))],
        out_specs=pl.BlockSpec((tile * ROW_TILES, LANES), lambda i, ti, te, nv: (i, 0)),
        scratch_shapes=[pltpu.VMEM((D_MODEL, d_ff), BF16), pltpu.VMEM((D_MODEL, d_ff), BF16),
                        pltpu.VMEM((d_ff, D_MODEL), BF16)],
    )
    return pl.pallas_call(
        _experts_body,
        grid_spec=grid_spec,
        out_shape=jax.ShapeDtypeStruct(xs.shape, F32),
        compiler_params=_params(("arbitrary",)),
        name="experts",
    )(tile_idx, tile_e, n_valid, xs, w_gate, w_up, w_down, b3(b_gate), b3(b_up), b3(b_down))


def _combine_body(pos_ref, next_pos_ref, wts_ref, h1_ref, g_ref, y_hbm, yp_ref, ys_ref, *scratch,
                  nb, n_blk, rows):
    bufs, y_ref, sem = (scratch[:TOP_K], scratch[TOP_K:2 * TOP_K]), scratch[2 * TOP_K], scratch[2 * TOP_K + 1]
    i = pl.program_id(0)

    def gather(p_ref, slot):
        def issue(r, c):
            for k in range(TOP_K):
                pltpu.make_async_copy(_tile_at(y_hbm, p_ref[r * TOP_K + k]),
                                      _tile_rows(bufs[slot][k], r), sem.at[slot]).start(priority=k % 2)
            return c
        lax.fori_loop(0, rows, issue, 0, unroll=ISSUE_UNROLL)

    def drain(slot):
        for k in range(TOP_K):
            pltpu.make_async_copy(_tile_rows(y_hbm, 0, rows), bufs[slot][k], sem.at[slot]).wait()

    def finish(slot):
        drain(slot)

        def group(j, c):
            r0 = pl.multiple_of(j * SUBLANES, SUBLANES)
            for dr in range(SUBLANES):
                for k in range(TOP_K):
                    pltpu.make_async_copy(_tile_at(y_hbm, next_pos_ref[(r0 + dr) * TOP_K + k]),
                                          _tile_rows(bufs[1 - slot][k], r0 + dr),
                                          sem.at[1 - slot]).start(priority=k % 2)
            w = wts_ref[pl.ds(r0, SUBLANES), :]
            gate = [jnp.broadcast_to(w[:, k:k + 1], (SUBLANES, LANES)) for k in range(TOP_K)]
            first = pl.multiple_of(r0 * ROW_TILES, ROW_TILES)
            moe = jnp.concatenate(
                [sum(gate[k] * bufs[slot][k][pl.ds(first + ch, SUBLANES, stride=ROW_TILES), :]
                     for k in range(TOP_K)) for ch in range(ROW_TILES)], axis=1)
            y_ref[pl.ds(r0, SUBLANES), :] = _rms(h1_ref[pl.ds(r0, SUBLANES), :] + moe, g_ref[...])
            return c

        lax.fori_loop(0, nb, group, 0, unroll=GROUP_UNROLL)

        @pl.when(i < n_blk)
        def _():
            yp_ref[...] = y_ref[...].reshape(nb, MERGE_T, D_MODEL)

        @pl.when(i == n_blk)
        def _():
            ys_ref[...] = y_ref[...]
            drain(1 - slot)

    @pl.when(i == 0)
    def _():
        gather(pos_ref, 0)

    for slot in range(2):
        @pl.when(i % 2 == slot)
        def _(slot=slot):
            finish(slot)


def _combine(pos, wts, h1, g, ys, nb, t):
    rows = nb * MERGE_T
    n_blk = t // MERGE_T
    return pl.pallas_call(
        functools.partial(_combine_body, nb=nb, n_blk=n_blk, rows=rows),
        grid=(n_blk + 1,),
        in_specs=[pl.BlockSpec((rows * TOP_K,), lambda i: (i,), memory_space=pltpu.SMEM),
                  pl.BlockSpec((rows * TOP_K,), lambda i: (jnp.minimum(i + 1, n_blk),),
                               memory_space=pltpu.SMEM),
                  pl.BlockSpec((rows, LANES), lambda i: (i, 0)),
                  pl.BlockSpec((rows, D_MODEL), lambda i: (i, 0)),
                  pl.BlockSpec((1, D_MODEL), lambda i: (0, 0)),
                  pl.BlockSpec(memory_space=pl.ANY)],
        out_specs=[pl.BlockSpec((nb, MERGE_T, D_MODEL), lambda i: (0, jnp.minimum(i, n_blk - 1), 0)),
                   pl.BlockSpec((rows, D_MODEL), lambda i: (0, 0))],
        out_shape=[jax.ShapeDtypeStruct((nb, t, D_MODEL), F32),
                   jax.ShapeDtypeStruct((rows, D_MODEL), F32)],
        scratch_shapes=[pltpu.VMEM((rows * ROW_TILES, LANES), F32)] * (2 * TOP_K)
                       + [pltpu.VMEM((rows, D_MODEL), F32), pltpu.SemaphoreType.DMA((2,))],
        compiler_params=_params(("arbitrary",)),
        name="combine",
    )(pos, pos, wts, h1, g, ys)


def _moe_plan(counts, idx, rank, tile):
    n_tok = idx.shape[0]
    n_tiles = (n_tok * TOP_K) // tile + N_EXPERTS
    padded = ((counts + tile - 1) // tile) * tile
    ends = jnp.cumsum(padded)
    off = ends - padded
    pos = ((off[idx] + rank) * ROW_TILES).reshape(-1)
    tile_ends = ends // tile
    n_valid = tile_ends[-1]
    tile_idx = jnp.minimum(jnp.arange(n_tiles, dtype=I32), n_valid - 1)
    tile_e = jnp.minimum(jnp.sum(tile_idx[:, None] >= tile_ends[None, :], axis=1), N_EXPERTS - 1).astype(I32)
    return pos.astype(I32), tile_idx, tile_e, n_valid.reshape(1).astype(I32), (ends - tile).astype(I32), \
        (counts > 0).astype(I32), n_tiles * tile


def kernel(x_prompt, x_sample, cache_k, cache_v, page_table, state_ssm_re, state_ssm_im, meta_tokens, g_mix, w_in, sb_bias, lambda_re, lambda_im, log_dt, b_re, b_im, c_re, c_im, d_skip, w_glu, b_glu, g_att_out, g_ssm_out, w_out, g_ffn, w_router, b_router, w_gate, b_gate, w_up, b_up, w_down, b_down, g_final):
    depth = g_mix.shape[0]
    assert depth == 1, "single-layer step"
    b, t, _ = x_prompt.shape
    db, ds, _ = x_sample.shape
    n_meta = meta_tokens.shape[0]
    assert ds == 1 and db == b * MERGE_T and t % BLOCK == 0 and n_meta <= BLOCK
    l = 0
    row = lambda a: a.reshape(1, -1)

    w_in_b = w_in[l].astype(BF16)
    g_mix_l = row(g_mix[l])
    q_p, k_p, v_p, kb_p, vb_p, u_p = _project(x_prompt, g_mix_l, w_in_b, PROJ_T)
    _, k_m, v_m, kb_m, vb_m, u_m = _project(meta_tokens[None], g_mix_l, w_in_b, n_meta)
    q_s, k_s, v_s, _, _, u_s = _project(x_sample.reshape(1, db, D_MODEL), g_mix_l, w_in_b, db)

    att_p = _attn_prompt(sb_bias[l], q_p, kb_p, vb_p, kb_m[0], vb_m[0], n_meta)
    n_phys, page = cache_k.shape[1], cache_k.shape[2]
    head_of_lane = jnp.arange(ATT_WIDTH) // HEAD_DIM
    q_bd = jnp.where(head_of_lane[None, None, :] == jnp.arange(N_HEADS)[None, :, None],
                     q_s[0][:, None, :], jnp.zeros((), BF16))
    keys_minor = lambda c: jnp.transpose(c, (0, 2, 3, 1)).reshape(n_phys, ATT_WIDTH, page)
    att_s = _attn_sample(page_table, q_bd, sb_bias[l].reshape(N_HEADS, 1),
                         keys_minor(cache_k[l]), keys_minor(cache_v[l]))

    ab_re, ab_im, bb_re, bb_im = _s5_prep(lambda_re[l], lambda_im[l], log_dt[l], b_re[l], b_im[l])
    bbd = jnp.concatenate([_block_diag_in(bb_re), _block_diag_in(bb_im)], axis=2)
    cre = _block_diag_out(c_re[l]).astype(BF16)
    cim = _block_diag_out(c_im[l]).astype(BF16)
    a_re = ab_re.reshape(2, 1, HALF_STATE)
    a_im = ab_im.reshape(2, 1, HALF_STATE)
    s5_w = (cre, cim, a_re, a_im, row(d_skip[l]), w_glu[l].astype(BF16), row(b_glu[l]), row(g_ssm_out[l]))
    zero_state = jnp.zeros((2, 2, b, HALF_STATE), F32)
    s_p, hf_p = _s5(u_p, u_m.reshape(n_meta, SSM_WIDTH), zero_state, bbd.astype(BF16), *s5_w)
    h0_s = jnp.stack([_state_in(state_ssm_re[l]), _state_in(state_ssm_im[l])], axis=1)
    s_s, hf_s = _s5(u_s.reshape(1, db, SSM_WIDTH), None, h0_s, bbd, *s5_w)

    h1, f3, idx, wts, rank, cnt = _merge(
        x_prompt, att_p, s_p, x_sample.reshape(db, D_MODEL), att_s.reshape(db, ATT_WIDTH),
        s_s.reshape(db, SSM_WIDTH), row(g_att_out[l]), w_out[l].astype(BF16), row(g_ffn[l]),
        w_router[l], row(b_router[l]))
    counts = cnt[0].astype(I32)
    pos, tile_idx, tile_e, n_valid, tail_start, has_rows, m_tot = _moe_plan(
        counts, idx[:, :TOP_K], rank[:, :TOP_K], EXPERT_TILE)
    rows = b * MERGE_T
    xs = _dispatch(tail_start, has_rows, n_valid, pos, f3, rows, m_tot, EXPERT_TILE)
    ys = _experts(tile_idx, tile_e, n_valid, xs, w_gate[l], w_up[l], w_down[l],
                  b_gate[l], b_up[l], b_down[l], EXPERT_TILE)
    y_prompt, y_sample = _combine(pos, wts, h1, row(g_final), ys, b, t)

    heads = lambda a: a.reshape(a.shape[:-1] + (N_HEADS, HEAD_DIM))
    with_meta = lambda m, p: heads(jnp.concatenate([jnp.broadcast_to(m, (b,) + m.shape[1:]), p], axis=1))[None]
    return (y_prompt, y_sample.reshape(db, ds, D_MODEL),
            with_meta(k_m, k_p), with_meta(v_m, v_p),
            heads(k_s.reshape(db, ds, ATT_WIDTH))[None], heads(v_s.reshape(db, ds, ATT_WIDTH))[None],
            _state_out(hf_p[:, 0])[None], _state_out(hf_p[:, 1])[None],
            _state_out(hf_s[:, 0])[None], _state_out(hf_s[:, 1])[None])
```
